```python
import math
import jax, jax.numpy as jnp
from jax import lax
import numpy as np

D_MODEL = 1024
BATCH = 8
SEQ = 2048
DEPTH = 4
DEC_BATCH = 128
DEC_SEQ = 8
PAST_LEN = 8192
PAGE_SIZE = 128

N_AB = (DEPTH + 1) // 2
N_C = DEPTH // 2

MLA_HEADS = 8
MLA_NOPE = 64
MLA_ROPE = 32
MLA_V = 64
MLA_Q_LORA = 768
MLA_KV_LORA = 256
ROPE_THETA = 10000.0
DIFF_HEADS = 4
DIFF_KV_HEADS = 2
DIFF_D = 64
NSA_HEADS = 16
NSA_GROUPS = 2
NSA_DK = 64
NSA_DV = 64
CMP_BLOCK = 32
CMP_STRIDE = 16
CMP_HIDDEN = 128
SEL_BLOCK = 64
SEL_TOP = 16
WINDOW = 512
Q_BLOCK = 128
SEL_Q_BLOCK = 32
EPS = 1e-6

AB_SPLIT = (MLA_Q_LORA, MLA_KV_LORA, MLA_ROPE, MLA_HEADS * MLA_V,
            DIFF_HEADS * 2 * DIFF_D, DIFF_KV_HEADS * 2 * DIFF_D, DIFF_KV_HEADS * 2 * DIFF_D, DIFF_HEADS * 2 * DIFF_D)
AB_OUT_W = MLA_HEADS * MLA_V + DIFF_HEADS * 2 * DIFF_D
C_SPLIT = (NSA_HEADS * NSA_DK,) + (NSA_GROUPS * NSA_DK,) * 6 + (3 * NSA_HEADS, NSA_HEADS * NSA_DV)
C_OUT_W = NSA_HEADS * NSA_DV

kernel_name = 'hybrid_mla_diff_nsa_decode_step'


def _rms(x, g):
    xf = x.astype(jnp.float32)
    y = xf * lax.rsqrt(jnp.mean(xf * xf, axis=-1, keepdims=True) + EPS)
    return (y * g.astype(jnp.float32)).astype(x.dtype)


def _split(x, sizes):
    return jnp.split(x, np.cumsum(sizes)[:-1].tolist(), axis=-1)


def _alibi_slopes(n):
    return 2.0 ** (-8.0 * jnp.arange(1, n + 1, dtype=jnp.float32) / n)


def _rope(x, pos):
    half = x.shape[-1] // 2
    freq = ROPE_THETA ** (-jnp.arange(half, dtype=jnp.float32) / half)
    ang = pos.astype(jnp.float32)[:, None] * freq[None, :]
    shape = (1, ang.shape[0]) + (1,) * (x.ndim - 3) + (half,)
    cos, sin = jnp.cos(ang).reshape(shape), jnp.sin(ang).reshape(shape)
    xf = x.astype(jnp.float32)
    x1, x2 = xf[..., :half], xf[..., half:]
    return jnp.concatenate([x1 * cos - x2 * sin, x2 * cos + x1 * sin], axis=-1).astype(x.dtype)


def _masked_softmax(s, mask):
    s = jnp.where(mask, s, -jnp.inf)
    m = jnp.max(s, axis=-1, keepdims=True)
    m = jnp.where(jnp.isfinite(m), m, 0.0)
    e = jnp.exp(s - m)
    d = jnp.sum(e, axis=-1, keepdims=True)
    return e / jnp.where(d > 0, d, 1.0)


def _sweep(fn, n_q, block=Q_BLOCK):
    qb = block if n_q % block == 0 else n_q
    nb = n_q // qb
    if nb == 1:
        return fn(0, qb)
    out = lax.map(lambda i: fn(i * qb, qb), jnp.arange(nb))
    out = jnp.moveaxis(out, 0, 1)
    return out.reshape((out.shape[0], nb * qb) + out.shape[3:])


def _gather_pages(cache, page_table, layer):
    g = cache[page_table, :, layer]
    return g.reshape((g.shape[0], g.shape[1] * g.shape[2]) + g.shape[3:])


def _mla(cq, ckv, kr, pos, kv_past, g_q, w_uq, g_kv, w_uk, w_uv):
    q = jnp.einsum('btc,chd->bthd', _rms(cq, g_q), w_uq)
    q_nope, q_rope = q[..., :MLA_NOPE], _rope(q[..., MLA_NOPE:], pos)
    kv_new = jnp.concatenate([_rms(ckv, g_kv), _rope(kr, pos)], axis=-1)
    kv_all = kv_new if kv_past is None else jnp.concatenate([kv_past, kv_new], axis=1)
    lat = kv_all[..., :MLA_KV_LORA]
    k_pos = jnp.arange(kv_all.shape[1])
    q_cat = jnp.concatenate([jnp.einsum('bthd,rhd->bthr', q_nope, w_uk), q_rope], axis=-1)
    scale = (MLA_NOPE + MLA_ROPE) ** -0.5

    def block(s0, qb):
        qc = lax.dynamic_slice_in_dim(q_cat, s0, qb, axis=1)
        qp = lax.dynamic_slice_in_dim(pos, s0, qb)
        s = jnp.einsum('bqhc,bkc->bhqk', qc, kv_all).astype(jnp.float32) * scale
        p = _masked_softmax(s, k_pos[None, :] <= qp[:, None])
        return jnp.einsum('bhqk,bkr->bqhr', p.astype(lat.dtype), lat)

    o_lat = _sweep(block, q_cat.shape[1])
    o = jnp.einsum('bthr,rhv->bthv', o_lat, w_uv)
    return o.reshape(o.shape[:2] + (-1,)), kv_new


def _diff(q, k, v, pos, kv_past, lam_p, g_sub, lam_init):
    B, T = q.shape[:2]
    G, R = DIFF_KV_HEADS, DIFF_HEADS // DIFF_KV_HEADS
    q = q.reshape(B, T, G, R, 2, DIFF_D)
    kv_new = jnp.stack([k.reshape(B, T, G, 2 * DIFF_D), v.reshape(B, T, G, 2 * DIFF_D)], axis=2)
    kv_all = kv_new if kv_past is None else jnp.concatenate([kv_past, kv_new], axis=1)
    Tk = kv_all.shape[1]
    k_all = kv_all[:, :, 0].reshape(B, Tk, G, 2, DIFF_D)
    v_all = kv_all[:, :, 1]
    k_pos = jnp.arange(Tk)
    lp = lam_p.astype(jnp.float32)
    lam = jnp.exp(jnp.sum(lp[0] * lp[1])) - jnp.exp(jnp.sum(lp[2] * lp[3])) + lam_init
    slopes = _alibi_slopes(DIFF_HEADS).reshape(1, G, R, 1, 1, 1)

    def block(s0, qb):
        qc = lax.dynamic_slice_in_dim(q, s0, qb, axis=1)
        qp = lax.dynamic_slice_in_dim(pos, s0, qb)
        dist = qp[:, None] - k_pos[None, :]
        s = jnp.einsum('bqgrcd,bkgcd->bgrcqk', qc, k_all).astype(jnp.float32) * (DIFF_D ** -0.5)
        p = _masked_softmax(s - slopes * dist.astype(jnp.float32), dist >= 0)
        a = p[:, :, :, 0] - lam * p[:, :, :, 1]
        return jnp.einsum('bgrqk,bkge->bqgre', a.astype(v_all.dtype), v_all)

    o = _sweep(block, T)
    o = _rms(o, g_sub) * (1.0 - lam_init)
    return o.reshape(B, T, -1), kv_new


def _compress(x, pe, w1, b1, w2):
    B, S, G, dk = x.shape
    nc = S // CMP_STRIDE
    xc = x[:, :nc * CMP_STRIDE].reshape(B, nc, CMP_STRIDE, G, dk)
    lo = jnp.einsum('bnjgd,jdh->bngh', xc + pe[:CMP_STRIDE, None, :], w1[:CMP_STRIDE])
    hi = jnp.einsum('bnjgd,jdh->bngh', xc + pe[CMP_STRIDE:, None, :], w1[CMP_STRIDE:])
    h = jax.nn.silu(lo[:, :-1] + hi[:, 1:] + b1)
    return jnp.einsum('bngh,hd->bngd', h, w2)


def _nsa(q, rows_new, win_new, gate_logits, pos, pos0, rows_past, win_past, pe, w1, b1, w2):
    B, T = q.shape[:2]
    G, R = NSA_GROUPS, NSA_HEADS // NSA_GROUPS
    q = q.reshape(B, T, G, R, NSA_DK)
    gates = jax.nn.sigmoid(gate_logits.astype(jnp.float32)).reshape(B, T, G, R, 3)
    rows = rows_new if rows_past is None else jnp.concatenate([rows_past, rows_new], axis=1)
    S = rows.shape[1]
    if win_past is None:
        win = jnp.pad(win_new, ((0, 0), (WINDOW, 0), (0, 0), (0, 0), (0, 0)))
        win_pos0 = -WINDOW
        win_state = win_new[:, T - min(WINDOW, T):]
    else:
        win = jnp.concatenate([win_past, win_new], axis=1)
        win_pos0 = pos0 - win_past.shape[1]
        win_state = win[:, T:]
    span = win.shape[1] - T
    k_cmp = _compress(rows[:, :, 0], pe[0], w1[0], b1[0], w2[0])
    v_cmp = _compress(rows[:, :, 1], pe[1], w1[1], b1[1], w2[1])
    n_cmp = k_cmp.shape[1]
    cmp_start = jnp.arange(n_cmp) * CMP_STRIDE
    cmp_end = cmp_start + CMP_BLOCK - 1
    n_sel = -(-S // SEL_BLOCK)
    sel_start = jnp.arange(n_sel) * SEL_BLOCK
    overlap = ((cmp_start[:, None] < sel_start[None, :] + SEL_BLOCK)
               & (cmp_end[:, None] >= sel_start[None, :])).astype(jnp.float32)
    sel = jnp.pad(rows[:, :, 2:], ((0, 0), (0, n_sel * SEL_BLOCK - S), (0, 0), (0, 0), (0, 0)))
    sel = sel.reshape(B, n_sel, SEL_BLOCK, 2, G, NSA_DK).transpose(0, 4, 3, 1, 2, 5)
    n_top = min(SEL_TOP, n_sel)
    slopes = _alibi_slopes(NSA_HEADS).reshape(G, R)
    scale = NSA_DK ** -0.5
    take_blocks = jax.vmap(jax.vmap(lambda blk, ix: blk[:, ix]))
    j_sel = jnp.arange(n_sel)

    def block(s0, qb):
        qc = lax.dynamic_slice_in_dim(q, s0, qb, axis=1)
        qp = lax.dynamic_slice_in_dim(pos, s0, qb)
        gc = lax.dynamic_slice_in_dim(gates, s0, qb, axis=1)
        dist_c = qp[:, None] - cmp_end[None, :]
        s = jnp.einsum('bqgrd,bngd->bgrqn', qc, k_cmp).astype(jnp.float32) * scale
        p_c = _masked_softmax(s - slopes[None, :, :, None, None] * dist_c.astype(jnp.float32), dist_c >= 0)
        o_c = jnp.einsum('bgrqn,bngd->bqgrd', p_c.astype(v_cmp.dtype), v_cmp)
        imp = jnp.einsum('bgrqn,ns->bgqs', p_c, overlap)
        cur = qp // SEL_BLOCK
        valid = sel_start[None, :] <= qp[:, None]
        forced = (j_sel[None, :] == 0) | (j_sel[None, :] == cur[:, None]) | (j_sel[None, :] == cur[:, None] - 1)
        score = jnp.where(valid, jnp.where(forced, jnp.inf, imp), -jnp.inf)
        _, idx = lax.top_k(score, n_top)
        kv_sel = take_blocks(sel, idx)
        kpos = idx[..., None] * SEL_BLOCK + jnp.arange(SEL_BLOCK)
        dist_s = qp[None, None, :, None, None] - kpos
        s = jnp.einsum('bqgrd,bgqnld->bgrqnl', qc, kv_sel[:, :, 0]).astype(jnp.float32) * scale
        s = s - slopes[None, :, :, None, None, None] * dist_s[:, :, None].astype(jnp.float32)
        sh = s.shape
        p_s = _masked_softmax(s.reshape(sh[:4] + (-1,)),
                              (dist_s >= 0)[:, :, None].reshape(sh[0], sh[1], 1, sh[3], -1)).reshape(sh)
        o_s = jnp.einsum('bgrqnl,bgqnld->bqgrd', p_s.astype(kv_sel.dtype), kv_sel[:, :, 1])
        w = lax.dynamic_slice_in_dim(win, s0, span + qb, axis=1)
        wpos = win_pos0 + s0 + jnp.arange(span + qb)
        dist_w = qp[:, None] - wpos[None, :]
        mask_w = (wpos[None, :] >= 0) & (dist_w >= 0) & (dist_w <= WINDOW)
        s = jnp.einsum('bqgrd,bkgd->bgrqk', qc, w[:, :, 0]).astype(jnp.float32) * scale
        p_w = _masked_softmax(s - slopes[None, :, :, None, None] * dist_w.astype(jnp.float32), mask_w)
        o_w = jnp.einsum('bgrqk,bkgd->bqgrd', p_w.astype(w.dtype), w[:, :, 1])
        return (gc[..., 0:1] * o_c + gc[..., 1:2] * o_s + gc[..., 2:3] * o_w).astype(q.dtype)

    o = _sweep(block, T, SEL_Q_BLOCK)
    return o.reshape(B, T, -1), win_state


def _run(x, c, p, cache_mla=None, cache_diff=None, cache_nsa=None, state_win=None, page_table=None):
    B, T, _ = x.shape
    has_past = page_table is not None
    pos0 = page_table.shape[1] * PAGE_SIZE if has_past else 0
    pos = pos0 + jnp.arange(T)
    mod = jnp.einsum('bc,lcd->lbd', jax.nn.silu(c), p['ada_w']) + p['ada_b'][:, None, :]
    mla_rows, diff_rows, nsa_rows, win_rows = [], [], [], []
    h = x
    for l in range(DEPTH):
        shift, scale, gate = jnp.split(mod[l][:, None, :], 3, axis=-1)
        u = _rms(h, p['pre_g'][l]) * (1.0 + scale) + shift
        i = l // 2
        if l % 2 == 0:
            cq, ckv, kr, ga, qd, kd, vd, gb = _split(u @ p['ab_w_in'][i], AB_SPLIT)
            o_a, kv_a = _mla(cq, ckv, kr, pos,
                             _gather_pages(cache_mla, page_table, i) if has_past else None,
                             p['mla_q_g'][i], p['mla_w_uq'][i], p['mla_kv_g'][i], p['mla_w_uk'][i], p['mla_w_uv'][i])
            o_b, kv_b = _diff(qd, kd, vd, pos,
                              _gather_pages(cache_diff, page_table, i) if has_past else None,
                              p['diff_lam'][i], p['diff_sub_g'][i], 0.8 - 0.6 * math.exp(-0.3 * l))
            y = jnp.concatenate([o_a * jax.nn.silu(ga), o_b * jax.nn.silu(gb)], axis=-1) @ p['ab_w_out'][i]
            mla_rows.append(kv_a)
            diff_rows.append(kv_b)
        else:
            qn, kc, vc, ks, vs, kw, vw, gl, gn = _split(u @ p['c_w_in'][i], C_SPLIT)
            grp = lambda t: t.reshape(B, T, NSA_GROUPS, NSA_DK)
            rows_new = jnp.stack([grp(kc), grp(vc), grp(ks), grp(vs)], axis=2)
            win_new = jnp.stack([grp(kw), grp(vw)], axis=2)
            o_n, win_state = _nsa(qn, rows_new, win_new, gl, pos, pos0,
                                  _gather_pages(cache_nsa, page_table, i) if has_past else None,
                                  state_win[:, i] if has_past else None,
                                  p['nsa_pe'][i], p['nsa_w1'][i], p['nsa_b1'][i], p['nsa_w2'][i])
            y = (o_n * jax.nn.silu(gn)) @ p['c_w_out'][i]
            nsa_rows.append(rows_new)
            win_rows.append(win_state)
        h = h + gate * _rms(y, p['post_g'][l])
    return (h, jnp.stack(mla_rows, axis=2), jnp.stack(diff_rows, axis=2),
            jnp.stack(nsa_rows, axis=2), jnp.stack(win_rows, axis=1))


def setup_inputs(seed: int = 0) -> dict:
    key = jax.random.key(seed)
    ks = jax.random.split(key, 32)
    n_pages = PAST_LEN // PAGE_SIZE
    n_used = DEC_BATCH * n_pages
    n_pool = n_used + n_used // 4
    wbuf = min(WINDOW, PAST_LEN)
    f32 = jnp.float32
    nrm = lambda k, shape, s=1.0: jax.random.normal(k, shape, f32) * s
    gain = lambda k, shape: 1.0 + 0.05 * jax.random.normal(k, shape, f32)
    page_table = jax.random.permutation(ks[8], n_pool)[:n_used].reshape(DEC_BATCH, n_pages).astype(jnp.int32)
    return {
        'x_prompt': nrm(ks[0], (BATCH, SEQ, D_MODEL)),
        'x_sample': nrm(ks[1], (DEC_BATCH, DEC_SEQ, D_MODEL)),
        'c_prompt': nrm(ks[2], (BATCH, D_MODEL)),
        'c_sample': nrm(ks[3], (DEC_BATCH, D_MODEL)),
        'cache_mla': nrm(ks[4], (n_pool, PAGE_SIZE, N_AB, MLA_KV_LORA + MLA_ROPE)),
        'cache_diff': nrm(ks[5], (n_pool, PAGE_SIZE, N_AB, 2, DIFF_KV_HEADS, 2 * DIFF_D)),
        'cache_nsa': nrm(ks[6], (n_pool, PAGE_SIZE, N_C, 4, NSA_GROUPS, NSA_DK)),
        'state_nsa_win': nrm(ks[7], (DEC_BATCH, N_C, wbuf, 2, NSA_GROUPS, NSA_DK)),
        'page_table': page_table,
        'ada_w': nrm(ks[9], (DEPTH, D_MODEL, 3 * D_MODEL), 0.5 * D_MODEL ** -0.5),
        'ada_b': nrm(ks[10], (DEPTH, 3 * D_MODEL), 0.02),
        'pre_g': gain(ks[11], (DEPTH, D_MODEL)),
        'post_g': gain(ks[12], (DEPTH, D_MODEL)),
        'ab_w_in': nrm(ks[13], (N_AB, D_MODEL, sum(AB_SPLIT)), D_MODEL ** -0.5),
        'mla_q_g': gain(ks[14], (N_AB, MLA_Q_LORA)),
        'mla_w_uq': nrm(ks[15], (N_AB, MLA_Q_LORA, MLA_HEADS, MLA_NOPE + MLA_ROPE), MLA_Q_LORA ** -0.5),
        'mla_kv_g': gain(ks[16], (N_AB, MLA_KV_LORA)),
        'mla_w_uk': nrm(ks[17], (N_AB, MLA_KV_LORA, MLA_HEADS, MLA_NOPE), MLA_KV_LORA ** -0.5),
        'mla_w_uv': nrm(ks[18], (N_AB, MLA_KV_LORA, MLA_HEADS, MLA_V), MLA_KV_LORA ** -0.5),
        'diff_lam': nrm(ks[19], (N_AB, 4, DIFF_D), 0.1),
        'diff_sub_g': gain(ks[20], (N_AB, 2 * DIFF_D)),
        'ab_w_out': nrm(ks[21], (N_AB, AB_OUT_W, D_MODEL), AB_OUT_W ** -0.5),
        'c_w_in': nrm(ks[22], (N_C, D_MODEL, sum(C_SPLIT)), D_MODEL ** -0.5),
        'nsa_pe': nrm(ks[23], (N_C, 2, CMP_BLOCK, NSA_DK), 0.5),
        'nsa_w1': nrm(ks[24], (N_C, 2, CMP_BLOCK, NSA_DK, CMP_HIDDEN), (CMP_BLOCK * NSA_DK) ** -0.5),
        'nsa_b1': nrm(ks[25], (N_C, 2, CMP_HIDDEN), 0.02),
        'nsa_w2': nrm(ks[26], (N_C, 2, CMP_HIDDEN, NSA_DK), CMP_HIDDEN ** -0.5),
        'c_w_out': nrm(ks[27], (N_C, C_OUT_W, D_MODEL), C_OUT_W ** -0.5),
    }


def reference(x_prompt, x_sample, c_prompt, c_sample, cache_mla, cache_diff, cache_nsa, state_nsa_win, page_table,
              ada_w, ada_b, pre_g, post_g, ab_w_in, mla_q_g, mla_w_uq, mla_kv_g, mla_w_uk, mla_w_uv,
              diff_lam, diff_sub_g, ab_w_out, c_w_in, nsa_pe, nsa_w1, nsa_b1, nsa_w2, c_w_out):
    p = dict(ada_w=ada_w, ada_b=ada_b, pre_g=pre_g, post_g=post_g, ab_w_in=ab_w_in,
             mla_q_g=mla_q_g, mla_w_uq=mla_w_uq, mla_kv_g=mla_kv_g, mla_w_uk=mla_w_uk, mla_w_uv=mla_w_uv,
             diff_lam=diff_lam, diff_sub_g=diff_sub_g, ab_w_out=ab_w_out, c_w_in=c_w_in,
             nsa_pe=nsa_pe, nsa_w1=nsa_w1, nsa_b1=nsa_b1, nsa_w2=nsa_w2, c_w_out=c_w_out)
    y_prompt, mla_p, diff_p, nsa_p, win_p = _run(x_prompt, c_prompt, p)
    y_sample, mla_s, diff_s, nsa_s, win_s = _run(x_sample, c_sample, p, cache_mla, cache_diff, cache_nsa,
                                                 state_nsa_win, page_table)
    return (y_prompt, y_sample, mla_p, mla_s, diff_p, diff_s, nsa_p, nsa_s, win_p, win_s)
```

```python
import math
from functools import partial

import jax
import jax.numpy as jnp
import numpy as np
from jax import lax
from jax.experimental import pallas as pl
from jax.experimental.pallas import tpu as pltpu

D_MODEL = 1024
DEPTH = 4
PAGE_SIZE = 128
MLA_HEADS = 8
MLA_NOPE = 64
MLA_ROPE = 32
MLA_V = 64
MLA_Q_LORA = 768
MLA_KV_LORA = 256
ROPE_THETA = 10000.0
DIFF_HEADS = 4
DIFF_KV_HEADS = 2
DIFF_D = 64
NSA_HEADS = 16
NSA_GROUPS = 2
NSA_DK = 64
NSA_DV = 64
CMP_BLOCK = 32
CMP_STRIDE = 16
CMP_HIDDEN = 128
SEL_BLOCK = 64
SEL_TOP = 16
WINDOW = 512
Q_BLOCK = 128
SEL_Q_BLOCK = 32
EPS = 1e-6

AB_SPLIT = (MLA_Q_LORA, MLA_KV_LORA, MLA_ROPE, MLA_HEADS * MLA_V,
            DIFF_HEADS * 2 * DIFF_D, DIFF_KV_HEADS * 2 * DIFF_D, DIFF_KV_HEADS * 2 * DIFF_D, DIFF_HEADS * 2 * DIFF_D)
C_SPLIT = (NSA_HEADS * NSA_DK,) + (NSA_GROUPS * NSA_DK,) * 6 + (3 * NSA_HEADS, NSA_HEADS * NSA_DV)

LANE = 128
ROW_TILE = 512
VMEM_LIMIT = 56 * 1024 * 1024


def _round_up(n, m):
    return -(-n // m) * m


def _inproj_body(h_ref, scale_ref, shift_ref, g_ref, w_ref, o_ref):
    bb, tt, d = h_ref.shape
    h = h_ref[...]
    y = h * lax.rsqrt(jnp.mean(h * h, axis=-1, keepdims=True) + EPS)
    u = (y * g_ref[...]) * (1.0 + scale_ref[...]) + shift_ref[...]
    u = u.reshape(bb * tt, d).astype(jnp.bfloat16)
    o = jnp.dot(u, w_ref[...], preferred_element_type=jnp.float32)
    o_ref[...] = o.reshape(bb, tt, o.shape[-1])


def _row_blocks(B, T):
    tt = min(T, ROW_TILE)
    bb = max(1, ROW_TILE // tt)
    bb = min(bb, B)
    assert T % tt == 0 and B % bb == 0
    return bb, tt


def _inproj(h, scale, shift, g, w_bf16):
    B, T, D = h.shape
    N = w_bf16.shape[1]
    bb, tt = _row_blocks(B, T)
    return pl.pallas_call(
        _inproj_body,
        grid=(B // bb, T // tt),
        in_specs=[
            pl.BlockSpec((bb, tt, D), lambda i, j: (i, j, 0)),
            pl.BlockSpec((bb, 1, D), lambda i, j: (i, 0, 0)),
            pl.BlockSpec((bb, 1, D), lambda i, j: (i, 0, 0)),
            pl.BlockSpec((1, 1, D), lambda i, j: (0, 0, 0)),
            pl.BlockSpec((D, N), lambda i, j: (0, 0)),
        ],
        out_specs=pl.BlockSpec((bb, tt, N), lambda i, j: (i, j, 0)),
        out_shape=jax.ShapeDtypeStruct((B, T, N), jnp.float32),
        compiler_params=pltpu.CompilerParams(
            dimension_semantics=("arbitrary", "arbitrary"), vmem_limit_bytes=VMEM_LIMIT),
        name="inproj",
    )(h, scale, shift, g, w_bf16)


def _outproj_body(o_ref, gsrc_ref, h_ref, gate_ref, pg_ref, w_ref, out_ref):
    bb, tt, d = h_ref.shape
    gs = gsrc_ref[...]
    a = o_ref[...] * (gs * jax.nn.sigmoid(gs))
    a = a.reshape(bb * tt, a.shape[-1]).astype(jnp.bfloat16)
    y = jnp.dot(a, w_ref[...], preferred_element_type=jnp.float32)
    y = y * lax.rsqrt(jnp.mean(y * y, axis=-1, keepdims=True) + EPS)
    y = y.reshape(bb, tt, d) * pg_ref[...]
    out_ref[...] = h_ref[...] + gate_ref[...] * y


def _outproj(o, gsrc, h, gate, pg, w_bf16):
    B, T, D = h.shape
    W = o.shape[-1]
    bb, tt = _row_blocks(B, T)
    return pl.pallas_call(
        _outproj_body,
        grid=(B // bb, T // tt),
        in_specs=[
            pl.BlockSpec((bb, tt, W), lambda i, j: (i, j, 0)),
            pl.BlockSpec((bb, tt, W), lambda i, j: (i, j, 0)),
            pl.BlockSpec((bb, tt, D), lambda i, j: (i, j, 0)),
            pl.BlockSpec((bb, 1, D), lambda i, j: (i, 0, 0)),
            pl.BlockSpec((1, 1, D), lambda i, j: (0, 0, 0)),
            pl.BlockSpec((W, D), lambda i, j: (0, 0)),
        ],
        out_specs=pl.BlockSpec((bb, tt, D), lambda i, j: (i, j, 0)),
        out_shape=jax.ShapeDtypeStruct((B, T, D), jnp.float32),
        compiler_params=pltpu.CompilerParams(
            dimension_semantics=("arbitrary", "arbitrary"), vmem_limit_bytes=VMEM_LIMIT),
        name="outproj",
    )(o, gsrc, h, gate, pg, w_bf16)


def _rms(x, g):
    xf = x.astype(jnp.float32)
    y = xf * lax.rsqrt(jnp.mean(xf * xf, axis=-1, keepdims=True) + EPS)
    return (y * g.astype(jnp.float32)).astype(x.dtype)


def _split(x, sizes):
    return jnp.split(x, np.cumsum(sizes)[:-1].tolist(), axis=-1)


def _alibi_slopes(n):
    return 2.0 ** (-8.0 * jnp.arange(1, n + 1, dtype=jnp.float32) / n)


def _rope(x, pos):
    half = x.shape[-1] // 2
    freq = ROPE_THETA ** (-jnp.arange(half, dtype=jnp.float32) / half)
    ang = pos.astype(jnp.float32)[:, None] * freq[None, :]
    shape = (1, ang.shape[0]) + (1,) * (x.ndim - 3) + (half,)
    cos, sin = jnp.cos(ang).reshape(shape), jnp.sin(ang).reshape(shape)
    xf = x.astype(jnp.float32)
    x1, x2 = xf[..., :half], xf[..., half:]
    return jnp.concatenate([x1 * cos - x2 * sin, x2 * cos + x1 * sin], axis=-1).astype(x.dtype)


def _masked_softmax(s, mask):
    s = jnp.where(mask, s, -jnp.inf)
    m = jnp.max(s, axis=-1, keepdims=True)
    m = jnp.where(jnp.isfinite(m), m, 0.0)
    e = jnp.exp(s - m)
    d = jnp.sum(e, axis=-1, keepdims=True)
    return e / jnp.where(d > 0, d, 1.0)


def _sweep(fn, n_q, block=Q_BLOCK):
    qb = block if n_q % block == 0 else n_q
    nb = n_q // qb
    if nb == 1:
        return fn(0, qb)
    out = lax.map(lambda i: fn(i * qb, qb), jnp.arange(nb))
    out = jnp.moveaxis(out, 0, 1)
    return out.reshape((out.shape[0], nb * qb) + out.shape[3:])


def _gather_pages(cache, page_table, layer):
    g = cache[page_table, :, layer]
    return g.reshape((g.shape[0], g.shape[1] * g.shape[2]) + g.shape[3:])


def _mla(cq, ckv, kr, pos, kv_past, g_q, w_uq, g_kv, w_uk, w_uv):
    q = jnp.einsum('btc,chd->bthd', _rms(cq, g_q), w_uq)
    q_nope, q_rope = q[..., :MLA_NOPE], _rope(q[..., MLA_NOPE:], pos)
    kv_new = jnp.concatenate([_rms(ckv, g_kv), _rope(kr, pos)], axis=-1)
    kv_all = kv_new if kv_past is None else jnp.concatenate([kv_past, kv_new], axis=1)
    lat = kv_all[..., :MLA_KV_LORA]
    k_pos = jnp.arange(kv_all.shape[1])
    q_cat = jnp.concatenate([jnp.einsum('bthd,rhd->bthr', q_nope, w_uk), q_rope], axis=-1)
    scale = (MLA_NOPE + MLA_ROPE) ** -0.5

    def block(s0, qb):
        qc = lax.dynamic_slice_in_dim(q_cat, s0, qb, axis=1)
        qp = lax.dynamic_slice_in_dim(pos, s0, qb)
        s = jnp.einsum('bqhc,bkc->bhqk', qc, kv_all).astype(jnp.float32) * scale
        p = _masked_softmax(s, k_pos[None, :] <= qp[:, None])
        return jnp.einsum('bhqk,bkr->bqhr', p.astype(lat.dtype), lat)

    o_lat = _sweep(block, q_cat.shape[1])
    o = jnp.einsum('bthr,rhv->bthv', o_lat, w_uv)
    return o.reshape(o.shape[:2] + (-1,)), kv_new


def _diff(q, k, v, pos, kv_past, lam_p, g_sub, lam_init):
    B, T = q.shape[:2]
    G, R = DIFF_KV_HEADS, DIFF_HEADS // DIFF_KV_HEADS
    q = q.reshape(B, T, G, R, 2, DIFF_D)
    kv_new = jnp.stack([k.reshape(B, T, G, 2 * DIFF_D), v.reshape(B, T, G, 2 * DIFF_D)], axis=2)
    kv_all = kv_new if kv_past is None else jnp.concatenate([kv_past, kv_new], axis=1)
    Tk = kv_all.shape[1]
    k_all = kv_all[:, :, 0].reshape(B, Tk, G, 2, DIFF_D)
    v_all = kv_all[:, :, 1]
    k_pos = jnp.arange(Tk)
    lp = lam_p.astype(jnp.float32)
    lam = jnp.exp(jnp.sum(lp[0] * lp[1])) - jnp.exp(jnp.sum(lp[2] * lp[3])) + lam_init
    slopes = _alibi_slopes(DIFF_HEADS).reshape(1, G, R, 1, 1, 1)

    def block(s0, qb):
        qc = lax.dynamic_slice_in_dim(q, s0, qb, axis=1)
        qp = lax.dynamic_slice_in_dim(pos, s0, qb)
        dist = qp[:, None] - k_pos[None, :]
        s = jnp.einsum('bqgrcd,bkgcd->bgrcqk', qc, k_all).astype(jnp.float32) * (DIFF_D ** -0.5)
        p = _masked_softmax(s - slopes * dist.astype(jnp.float32), dist >= 0)
        a = p[:, :, :, 0] - lam * p[:, :, :, 1]
        return jnp.einsum('bgrqk,bkge->bqgre', a.astype(v_all.dtype), v_all)

    o = _sweep(block, T)
    o = _rms(o, g_sub) * (1.0 - lam_init)
    return o.reshape(B, T, -1), kv_new


def _compress(x, pe, w1, b1, w2):
    B, S, G, dk = x.shape
    nc = S // CMP_STRIDE
    xc = x[:, :nc * CMP_STRIDE].reshape(B, nc, CMP_STRIDE, G, dk)
    lo = jnp.einsum('bnjgd,jdh->bngh', xc + pe[:CMP_STRIDE, None, :], w1[:CMP_STRIDE])
    hi = jnp.einsum('bnjgd,jdh->bngh', xc + pe[CMP_STRIDE:, None, :], w1[CMP_STRIDE:])
    h = jax.nn.silu(lo[:, :-1] + hi[:, 1:] + b1)
    return jnp.einsum('bngh,hd->bngd', h, w2)


def _nsa(q, rows_new, win_new, gate_logits, pos, pos0, rows_past, win_past, pe, w1, b1, w2):
    B, T = q.shape[:2]
    G, R = NSA_GROUPS, NSA_HEADS // NSA_GROUPS
    q = q.reshape(B, T, G, R, NSA_DK)
    gates = jax.nn.sigmoid(gate_logits.astype(jnp.float32)).reshape(B, T, G, R, 3)
    rows = rows_new if rows_past is None else jnp.concatenate([rows_past, rows_new], axis=1)
    S = rows.shape[1]
    if win_past is None:
        win = jnp.pad(win_new, ((0, 0), (WINDOW, 0), (0, 0), (0, 0), (0, 0)))
        win_pos0 = -WINDOW
        win_state = win_new[:, T - min(WINDOW, T):]
    else:
        win = jnp.concatenate([win_past, win_new], axis=1)
        win_pos0 = pos0 - win_past.shape[1]
        win_state = win[:, T:]
    span = win.shape[1] - T
    k_cmp = _compress(rows[:, :, 0], pe[0], w1[0], b1[0], w2[0])
    v_cmp = _compress(rows[:, :, 1], pe[1], w1[1], b1[1], w2[1])
    n_cmp = k_cmp.shape[1]
    cmp_start = jnp.arange(n_cmp) * CMP_STRIDE
    cmp_end = cmp_start + CMP_BLOCK - 1
    n_sel = -(-S // SEL_BLOCK)
    sel_start = jnp.arange(n_sel) * SEL_BLOCK
    overlap = ((cmp_start[:, None] < sel_start[None, :] + SEL_BLOCK)
               & (cmp_end[:, None] >= sel_start[None, :])).astype(jnp.float32)
    sel = jnp.pad(rows[:, :, 2:], ((0, 0), (0, n_sel * SEL_BLOCK - S), (0, 0), (0, 0), (0, 0)))
    sel = sel.reshape(B, n_sel, SEL_BLOCK, 2, G, NSA_DK).transpose(0, 4, 3, 1, 2, 5)
    n_top = min(SEL_TOP, n_sel)
    slopes = _alibi_slopes(NSA_HEADS).reshape(G, R)
    scale = NSA_DK ** -0.5
    take_blocks = jax.vmap(jax.vmap(lambda blk, ix: blk[:, ix]))
    j_sel = jnp.arange(n_sel)

    def block(s0, qb):
        qc = lax.dynamic_slice_in_dim(q, s0, qb, axis=1)
        qp = lax.dynamic_slice_in_dim(pos, s0, qb)
        gc = lax.dynamic_slice_in_dim(gates, s0, qb, axis=1)
        dist_c = qp[:, None] - cmp_end[None, :]
        s = jnp.einsum('bqgrd,bngd->bgrqn', qc, k_cmp).astype(jnp.float32) * scale
        p_c = _masked_softmax(s - slopes[None, :, :, None, None] * dist_c.astype(jnp.float32), dist_c >= 0)
        o_c = jnp.einsum('bgrqn,bngd->bqgrd', p_c.astype(v_cmp.dtype), v_cmp)
        imp = jnp.einsum('bgrqn,ns->bgqs', p_c, overlap)
        cur = qp // SEL_BLOCK
        valid = sel_start[None, :] <= qp[:, None]
        forced = (j_sel[None, :] == 0) | (j_sel[None, :] == cur[:, None]) | (j_sel[None, :] == cur[:, None] - 1)
        score = jnp.where(valid, jnp.where(forced, jnp.inf, imp), -jnp.inf)
        _, idx = lax.top_k(score, n_top)
        kv_sel = take_blocks(sel, idx)
        kpos = idx[..., None] * SEL_BLOCK + jnp.arange(SEL_BLOCK)
        dist_s = qp[None, None, :, None, None] - kpos
        s = jnp.einsum('bqgrd,bgqnld->bgrqnl', qc, kv_sel[:, :, 0]).astype(jnp.float32) * scale
        s = s - slopes[None, :, :, None, None, None] * dist_s[:, :, None].astype(jnp.float32)
        sh = s.shape
        p_s = _masked_softmax(s.reshape(sh[:4] + (-1,)),
                              (dist_s >= 0)[:, :, None].reshape(sh[0], sh[1], 1, sh[3], -1)).reshape(sh)
        o_s = jnp.einsum('bgrqnl,bgqnld->bqgrd', p_s.astype(kv_sel.dtype), kv_sel[:, :, 1])
        w = lax.dynamic_slice_in_dim(win, s0, span + qb, axis=1)
        wpos = win_pos0 + s0 + jnp.arange(span + qb)
        dist_w = qp[:, None] - wpos[None, :]
        mask_w = (wpos[None, :] >= 0) & (dist_w >= 0) & (dist_w <= WINDOW)
        s = jnp.einsum('bqgrd,bkgd->bgrqk', qc, w[:, :, 0]).astype(jnp.float32) * scale
        p_w = _masked_softmax(s - slopes[None, :, :, None, None] * dist_w.astype(jnp.float32), mask_w)
        o_w = jnp.einsum('bgrqk,bkgd->bqgrd', p_w.astype(w.dtype), w[:, :, 1])
        return (gc[..., 0:1] * o_c + gc[..., 1:2] * o_s + gc[..., 2:3] * o_w).astype(q.dtype)

    o = _sweep(block, T, SEL_Q_BLOCK)
    return o.reshape(B, T, -1), win_state


def _pad_cols(w, n):
    return jnp.pad(w, ((0, 0), (0, n - w.shape[1])))


def _run(x, c, p, cache_mla=None, cache_diff=None, cache_nsa=None, state_win=None, page_table=None):
    B, T, _ = x.shape
    has_past = page_table is not None
    pos0 = page_table.shape[1] * PAGE_SIZE if has_past else 0
    pos = pos0 + jnp.arange(T)
    mod = jnp.einsum('bc,lcd->lbd', jax.nn.silu(c), p['ada_w']) + p['ada_b'][:, None, :]
    mla_rows, diff_rows, nsa_rows, win_rows = [], [], [], []
    h = x
    n_ab, n_c = sum(AB_SPLIT), sum(C_SPLIT)
    for l in range(DEPTH):
        shift, scale, gate = jnp.split(mod[l][:, None, :], 3, axis=-1)
        pre_g = p['pre_g'][l].reshape(1, 1, D_MODEL)
        post_g = p['post_g'][l].reshape(1, 1, D_MODEL)
        i = l // 2
        if l % 2 == 0:
            w_in = _pad_cols(p['ab_w_in'][i], _round_up(n_ab, LANE)).astype(jnp.bfloat16)
            proj = _inproj(h, scale, shift, pre_g, w_in)[..., :n_ab]
            cq, ckv, kr, ga, qd, kd, vd, gb = _split(proj, AB_SPLIT)
            o_a, kv_a = _mla(cq, ckv, kr, pos,
                             _gather_pages(cache_mla, page_table, i) if has_past else None,
                             p['mla_q_g'][i], p['mla_w_uq'][i], p['mla_kv_g'][i], p['mla_w_uk'][i], p['mla_w_uv'][i])
            o_b, kv_b = _diff(qd, kd, vd, pos,
                              _gather_pages(cache_diff, page_table, i) if has_past else None,
                              p['diff_lam'][i], p['diff_sub_g'][i], 0.8 - 0.6 * math.exp(-0.3 * l))
            o = jnp.concatenate([o_a, o_b], axis=-1)
            gsrc = jnp.concatenate([ga, gb], axis=-1)
            h = _outproj(o, gsrc, h, gate, post_g, p['ab_w_out'][i].astype(jnp.bfloat16))
            mla_rows.append(kv_a)
            diff_rows.append(kv_b)
        else:
            w_in = _pad_cols(p['c_w_in'][i], _round_up(n_c, LANE)).astype(jnp.bfloat16)
            proj = _inproj(h, scale, shift, pre_g, w_in)[..., :n_c]
            qn, kc, vc, ks, vs, kw, vw, gl, gn = _split(proj, C_SPLIT)
            grp = lambda t: t.reshape(B, T, NSA_GROUPS, NSA_DK)
            rows_new = jnp.stack([grp(kc), grp(vc), grp(ks), grp(vs)], axis=2)
            win_new = jnp.stack([grp(kw), grp(vw)], axis=2)
            o_n, win_state = _nsa(qn, rows_new, win_new, gl, pos, pos0,
                                  _gather_pages(cache_nsa, page_table, i) if has_past else None,
                                  state_win[:, i] if has_past else None,
                                  p['nsa_pe'][i], p['nsa_w1'][i], p['nsa_b1'][i], p['nsa_w2'][i])
            h = _outproj(o_n, gn, h, gate, post_g, p['c_w_out'][i].astype(jnp.bfloat16))
            nsa_rows.append(rows_new)
            win_rows.append(win_state)
    return (h, jnp.stack(mla_rows, axis=2), jnp.stack(diff_rows, axis=2),
            jnp.stack(nsa_rows, axis=2), jnp.stack(win_rows, axis=1))


def kernel(x_prompt, x_sample, c_prompt, c_sample, cache_mla, cache_diff, cache_nsa, state_nsa_win, page_table,
           ada_w, ada_b, pre_g, post_g, ab_w_in, mla_q_g, mla_w_uq, mla_kv_g, mla_w_uk, mla_w_uv,
           diff_lam, diff_sub_g, ab_w_out, c_w_in, nsa_pe, nsa_w1, nsa_b1, nsa_w2, c_w_out):
    p = dict(ada_w=ada_w, ada_b=ada_b, pre_g=pre_g, post_g=post_g, ab_w_in=ab_w_in,
             mla_q_g=mla_q_g, mla_w_uq=mla_w_uq, mla_kv_g=mla_kv_g, mla_w_uk=mla_w_uk, mla_w_uv=mla_w_uv,
             diff_lam=diff_lam, diff_sub_g=diff_sub_g, ab_w_out=ab_w_out, c_w_in=c_w_in,
             nsa_pe=nsa_pe, nsa_w1=nsa_w1, nsa_b1=nsa_b1, nsa_w2=nsa_w2, c_w_out=c_w_out)
    y_prompt, mla_p, diff_p, nsa_p, win_p = _run(x_prompt, c_prompt, p)
    y_sample, mla_s, diff_s, nsa_s, win_s = _run(x_sample, c_sample, p, cache_mla, cache_diff, cache_nsa,
                                                 state_nsa_win, page_table)
    return (y_prompt, y_sample, mla_p, mla_s, diff_p, diff_s, nsa_p, nsa_s, win_p, win_s)
```

```python
import math
from functools import partial

import jax
import jax.numpy as jnp
import numpy as np
from jax import lax
from jax.experimental import pallas as pl
from jax.experimental.pallas import tpu as pltpu

D_MODEL = 1024
DEPTH = 4
PAGE_SIZE = 128
MLA_HEADS = 8
MLA_NOPE = 64
MLA_ROPE = 32
MLA_V = 64
MLA_Q_LORA = 768
MLA_KV_LORA = 256
ROPE_THETA = 10000.0
DIFF_HEADS = 4
DIFF_KV_HEADS = 2
DIFF_D = 64
NSA_HEADS = 16
NSA_GROUPS = 2
NSA_DK = 64
NSA_DV = 64
CMP_BLOCK = 32
CMP_STRIDE = 16
CMP_HIDDEN = 128
SEL_BLOCK = 64
SEL_TOP = 16
WINDOW = 512
Q_BLOCK = 128
SEL_Q_BLOCK = 32
EPS = 1e-6

AB_SPLIT = (MLA_Q_LORA, MLA_KV_LORA, MLA_ROPE, MLA_HEADS * MLA_V,
            DIFF_HEADS * 2 * DIFF_D, DIFF_KV_HEADS * 2 * DIFF_D, DIFF_KV_HEADS * 2 * DIFF_D, DIFF_HEADS * 2 * DIFF_D)
C_SPLIT = (NSA_HEADS * NSA_DK,) + (NSA_GROUPS * NSA_DK,) * 6 + (3 * NSA_HEADS, NSA_HEADS * NSA_DV)

LANE = 128
ROW_TILE = 512
VMEM_LIMIT = 56 * 1024 * 1024


def _round_up(n, m):
    return -(-n // m) * m


def _inproj_body(h_ref, scale_ref, shift_ref, g_ref, w_ref, o_ref):
    bb, tt, d = h_ref.shape
    h = h_ref[...]
    y = h * lax.rsqrt(jnp.mean(h * h, axis=-1, keepdims=True) + EPS)
    u = (y * g_ref[...]) * (1.0 + scale_ref[...]) + shift_ref[...]
    u = u.reshape(bb * tt, d).astype(jnp.bfloat16)
    o = jnp.dot(u, w_ref[...], preferred_element_type=jnp.float32)
    o_ref[...] = o.reshape(bb, tt, o.shape[-1])


def _row_blocks(B, T):
    tt = min(T, ROW_TILE)
    bb = max(1, ROW_TILE // tt)
    bb = min(bb, B)
    assert T % tt == 0 and B % bb == 0
    return bb, tt


def _inproj(h, scale, shift, g, w_bf16):
    B, T, D = h.shape
    N = w_bf16.shape[1]
    bb, tt = _row_blocks(B, T)
    return pl.pallas_call(
        _inproj_body,
        grid=(B // bb, T // tt),
        in_specs=[
            pl.BlockSpec((bb, tt, D), lambda i, j: (i, j, 0)),
            pl.BlockSpec((bb, 1, D), lambda i, j: (i, 0, 0)),
            pl.BlockSpec((bb, 1, D), lambda i, j: (i, 0, 0)),
            pl.BlockSpec((1, 1, D), lambda i, j: (0, 0, 0)),
            pl.BlockSpec((D, N), lambda i, j: (0, 0)),
        ],
        out_specs=pl.BlockSpec((bb, tt, N), lambda i, j: (i, j, 0)),
        out_shape=jax.ShapeDtypeStruct((B, T, N), jnp.float32),
        compiler_params=pltpu.CompilerParams(
            dimension_semantics=("arbitrary", "arbitrary"), vmem_limit_bytes=VMEM_LIMIT),
        name="inproj",
    )(h, scale, shift, g, w_bf16)


def _outproj_body(o_ref, gsrc_ref, h_ref, gate_ref, pg_ref, w_ref, out_ref):
    bb, tt, d = h_ref.shape
    gs = gsrc_ref[...]
    a = o_ref[...] * (gs * jax.nn.sigmoid(gs))
    a = a.reshape(bb * tt, a.shape[-1]).astype(jnp.bfloat16)
    y = jnp.dot(a, w_ref[...], preferred_element_type=jnp.float32)
    y = y * lax.rsqrt(jnp.mean(y * y, axis=-1, keepdims=True) + EPS)
    y = y.reshape(bb, tt, d) * pg_ref[...]
    out_ref[...] = h_ref[...] + gate_ref[...] * y


def _outproj(o, gsrc, h, gate, pg, w_bf16):
    B, T, D = h.shape
    W = o.shape[-1]
    bb, tt = _row_blocks(B, T)
    return pl.pallas_call(
        _outproj_body,
        grid=(B // bb, T // tt),
        in_specs=[
            pl.BlockSpec((bb, tt, W), lambda i, j: (i, j, 0)),
            pl.BlockSpec((bb, tt, W), lambda i, j: (i, j, 0)),
            pl.BlockSpec((bb, tt, D), lambda i, j: (i, j, 0)),
            pl.BlockSpec((bb, 1, D), lambda i, j: (i, 0, 0)),
            pl.BlockSpec((1, 1, D), lambda i, j: (0, 0, 0)),
            pl.BlockSpec((W, D), lambda i, j: (0, 0)),
        ],
        out_specs=pl.BlockSpec((bb, tt, D), lambda i, j: (i, j, 0)),
        out_shape=jax.ShapeDtypeStruct((B, T, D), jnp.float32),
        compiler_params=pltpu.CompilerParams(
            dimension_semantics=("arbitrary", "arbitrary"), vmem_limit_bytes=VMEM_LIMIT),
        name="outproj",
    )(o, gsrc, h, gate, pg, w_bf16)


NSA_TQ = 128
NSA_TK = 256
NEG_INF = float("-inf")


def _dot_nt(a, b):
    return lax.dot_general(a, b, (((1,), (1,)), ((), ())), preferred_element_type=jnp.float32)


def _dot_split3(a, b01):
    hi = a.astype(jnp.bfloat16)
    r1 = a - hi.astype(jnp.float32)
    mid = r1.astype(jnp.bfloat16)
    lo = (r1 - mid.astype(jnp.float32)).astype(jnp.bfloat16)
    dot = partial(jnp.dot, preferred_element_type=jnp.float32)
    return dot(hi, b01) + dot(mid, b01) + dot(lo, b01)


def _flash_step(q, k, v, slope3, dist, mask, carry):
    m, l, acc = carry
    R, TQ, _ = m.shape
    s = _dot_nt(q, k.astype(jnp.bfloat16))
    s3 = s.reshape(R, TQ, s.shape[-1]) - slope3 * dist[None]
    s3 = jnp.where(mask[None], s3, NEG_INF)
    m_new = jnp.maximum(m, jnp.max(s3, axis=-1, keepdims=True))
    m_safe = jnp.where(m_new == NEG_INF, 0.0, m_new)
    p = jnp.exp(s3 - m_safe)
    alpha = jnp.exp(m - m_safe)
    l = alpha * l + jnp.sum(p, axis=-1, keepdims=True)
    pv = jnp.dot(p.reshape(R * TQ, p.shape[-1]).astype(jnp.bfloat16), v.astype(jnp.bfloat16),
                 preferred_element_type=jnp.float32)
    acc = alpha * acc + pv.reshape(R, TQ, pv.shape[-1])
    return m_new, l, acc


def _flash_init(R, TQ, dv):
    return (jnp.full((R, TQ, 1), NEG_INF, jnp.float32), jnp.zeros((R, TQ, 1), jnp.float32),
            jnp.zeros((R, TQ, dv), jnp.float32))


def _flash_out(carry):
    _, l, acc = carry
    return acc / jnp.where(l > 0, l, 1.0)


def _select_blocks(imp, qpos, n_top):
    TQ, n_sel = imp.shape
    j = lax.broadcasted_iota(jnp.int32, (TQ, n_sel), 1)
    cur = qpos // SEL_BLOCK
    valid = j * SEL_BLOCK <= qpos
    forced = (j == 0) | (j == cur) | (j == cur - 1)
    score = jnp.where(valid, jnp.where(forced, jnp.inf, imp), NEG_INF)
    rank = jnp.zeros((TQ, n_sel), jnp.int32)
    for k in range(n_sel):
        col = score[:, k:k + 1]
        beats = (col > score) | ((col == score) & (j > k))
        rank = rank + beats.astype(jnp.int32)
    return (rank < n_top).astype(jnp.float32)


def _nsa_prompt_body(q_ref, kc_ref, vc_ref, ks_ref, vs_ref, kw_ref, vw_ref, gl_ref, slope_ref, ov_ref, et_ref,
                     o_ref, *, n_top):
    _, _, R, TQ, dk = q_ref.shape
    TK = NSA_TK
    qt = pl.program_id(2)
    s0 = qt * TQ
    q = (q_ref[0, 0].reshape(R * TQ, dk) * (NSA_DK ** -0.5)).astype(jnp.bfloat16)
    slope3 = slope_ref[0]
    qpos = s0 + lax.broadcasted_iota(jnp.int32, (TQ, 1), 0)

    n_cmp = kc_ref.shape[2]
    cmp_end = lax.broadcasted_iota(jnp.int32, (1, n_cmp), 1) * CMP_STRIDE + (CMP_BLOCK - 1)
    dist_c = qpos - cmp_end
    s = _dot_nt(q, kc_ref[0, 0].astype(jnp.bfloat16))
    s3 = s.reshape(R, TQ, n_cmp) - slope3 * dist_c.astype(jnp.float32)[None]
    s3 = jnp.where((dist_c >= 0)[None], s3, NEG_INF)
    m_c = jnp.max(s3, axis=-1, keepdims=True)
    e_c = jnp.exp(s3 - jnp.where(m_c == NEG_INF, 0.0, m_c))
    d_c = jnp.sum(e_c, axis=-1, keepdims=True)
    p_c = e_c / jnp.where(d_c > 0, d_c, 1.0)
    o_c = jnp.dot(p_c.reshape(R * TQ, n_cmp).astype(jnp.bfloat16), vc_ref[0, 0].astype(jnp.bfloat16),
                  preferred_element_type=jnp.float32).reshape(R, TQ, dk)
    imp = _dot_split3(jnp.sum(p_c, axis=0), ov_ref[...])
    selmask = _select_blocks(imp, qpos, n_top).astype(jnp.bfloat16)

    def sel_step(kt, carry):
        k0 = pl.multiple_of(kt * TK, TK)
        kpos = k0 + lax.broadcasted_iota(jnp.int32, (1, TK), 1)
        dist = qpos - kpos
        picked = _dot_nt(selmask, et_ref[pl.ds(k0, TK), :]) > 0.5
        return _flash_step(q, ks_ref[0, 0, pl.ds(k0, TK), :], vs_ref[0, 0, pl.ds(k0, TK), :], slope3,
                           dist.astype(jnp.float32), picked & (dist >= 0), carry)

    n_kt = (s0 + TQ + TK - 1) // TK
    o_s = _flash_out(lax.fori_loop(0, n_kt, sel_step, _flash_init(R, TQ, dk)))

    def win_step(kt, carry):
        k0 = pl.multiple_of(kt * TK, TK)
        kpos = k0 + lax.broadcasted_iota(jnp.int32, (1, TK), 1)
        dist = qpos - kpos
        return _flash_step(q, kw_ref[0, 0, pl.ds(k0, TK), :], vw_ref[0, 0, pl.ds(k0, TK), :], slope3,
                           dist.astype(jnp.float32), (dist >= 0) & (dist <= WINDOW), carry)

    kt_lo = jnp.maximum(s0 - WINDOW, 0) // TK
    o_w = _flash_out(lax.fori_loop(kt_lo, n_kt, win_step, _flash_init(R, TQ, dk)))

    gates = jax.nn.sigmoid(gl_ref[0, 0])
    o_ref[0, 0] = gates[..., 0:1] * o_c + gates[..., 1:2] * o_s + gates[..., 2:3] * o_w


def _nsa_prompt(q, k_cmp, v_cmp, ks, vs, kw, vw, gl):
    B, T, _ = q.shape
    G, R, dk = NSA_GROUPS, NSA_HEADS // NSA_GROUPS, NSA_DK
    TQ, TK = NSA_TQ, NSA_TK
    assert T % TK == 0 and TK % TQ == 0 and TK % SEL_BLOCK == 0
    n_sel = T // SEL_BLOCK
    n_top = min(SEL_TOP, n_sel)
    n_cmp = k_cmp.shape[1]
    n_cmp_pad = _round_up(n_cmp, LANE)
    q5 = q.reshape(B, T, G, R, dk).transpose(0, 2, 3, 1, 4)
    gl5 = gl.reshape(B, T, G, R, 3).transpose(0, 2, 3, 1, 4)
    by_group = lambda t: t.reshape(B, -1, G, dk).transpose(0, 2, 1, 3)
    pad_cmp = lambda t: jnp.pad(by_group(t.reshape(B, n_cmp, G * dk)), ((0, 0), (0, 0), (0, n_cmp_pad - n_cmp), (0, 0)))
    slopes = _alibi_slopes(NSA_HEADS).reshape(G, R, 1, 1)
    cmp_start = np.arange(n_cmp_pad) * CMP_STRIDE
    sel_start = np.arange(n_sel) * SEL_BLOCK
    overlap = ((cmp_start[:, None] < sel_start[None, :] + SEL_BLOCK)
               & (cmp_start[:, None] + CMP_BLOCK - 1 >= sel_start[None, :]) & (np.arange(n_cmp_pad)[:, None] < n_cmp))
    expand_t = (np.arange(T)[:, None] // SEL_BLOCK) == np.arange(n_sel)[None, :]
    row_spec = lambda rows: pl.BlockSpec((1, 1, rows, dk), lambda b, g, t: (b, g, 0, 0))
    out = pl.pallas_call(
        partial(_nsa_prompt_body, n_top=n_top),
        grid=(B, G, T // TQ),
        in_specs=[
            pl.BlockSpec((1, 1, R, TQ, dk), lambda b, g, t: (b, g, 0, t, 0)),
            row_spec(n_cmp_pad), row_spec(n_cmp_pad), row_spec(T), row_spec(T), row_spec(T), row_spec(T),
            pl.BlockSpec((1, 1, R, TQ, 3), lambda b, g, t: (b, g, 0, t, 0)),
            pl.BlockSpec((1, R, 1, 1), lambda b, g, t: (g, 0, 0, 0)),
            pl.BlockSpec((n_cmp_pad, n_sel), lambda b, g, t: (0, 0)),
            pl.BlockSpec((T, n_sel), lambda b, g, t: (0, 0)),
        ],
        out_specs=pl.BlockSpec((1, 1, R, TQ, dk), lambda b, g, t: (b, g, 0, t, 0)),
        out_shape=jax.ShapeDtypeStruct((B, G, R, T, dk), jnp.float32),
        compiler_params=pltpu.CompilerParams(
            dimension_semantics=("arbitrary", "arbitrary", "arbitrary"), vmem_limit_bytes=VMEM_LIMIT),
        name="nsa_prompt",
    )(q5, pad_cmp(k_cmp), pad_cmp(v_cmp), by_group(ks), by_group(vs), by_group(kw), by_group(vw), gl5, slopes,
      jnp.asarray(overlap, jnp.bfloat16), jnp.asarray(expand_t, jnp.bfloat16))
    return out.transpose(0, 3, 1, 2, 4).reshape(B, T, G * R * dk)


def _rms(x, g):
    xf = x.astype(jnp.float32)
    y = xf * lax.rsqrt(jnp.mean(xf * xf, axis=-1, keepdims=True) + EPS)
    return (y * g.astype(jnp.float32)).astype(x.dtype)


def _split(x, sizes):
    return jnp.split(x, np.cumsum(sizes)[:-1].tolist(), axis=-1)


def _alibi_slopes(n):
    return 2.0 ** (-8.0 * jnp.arange(1, n + 1, dtype=jnp.float32) / n)


def _rope(x, pos):
    half = x.shape[-1] // 2
    freq = ROPE_THETA ** (-jnp.arange(half, dtype=jnp.float32) / half)
    ang = pos.astype(jnp.float32)[:, None] * freq[None, :]
    shape = (1, ang.shape[0]) + (1,) * (x.ndim - 3) + (half,)
    cos, sin = jnp.cos(ang).reshape(shape), jnp.sin(ang).reshape(shape)
    xf = x.astype(jnp.float32)
    x1, x2 = xf[..., :half], xf[..., half:]
    return jnp.concatenate([x1 * cos - x2 * sin, x2 * cos + x1 * sin], axis=-1).astype(x.dtype)


def _masked_softmax(s, mask):
    s = jnp.where(mask, s, -jnp.inf)
    m = jnp.max(s, axis=-1, keepdims=True)
    m = jnp.where(jnp.isfinite(m), m, 0.0)
    e = jnp.exp(s - m)
    d = jnp.sum(e, axis=-1, keepdims=True)
    return e / jnp.where(d > 0, d, 1.0)


def _sweep(fn, n_q, block=Q_BLOCK):
    qb = block if n_q % block == 0 else n_q
    nb = n_q // qb
    if nb == 1:
        return fn(0, qb)
    out = lax.map(lambda i: fn(i * qb, qb), jnp.arange(nb))
    out = jnp.moveaxis(out, 0, 1)
    return out.reshape((out.shape[0], nb * qb) + out.shape[3:])


def _gather_pages(cache, page_table, layer):
    g = cache[page_table, :, layer]
    return g.reshape((g.shape[0], g.shape[1] * g.shape[2]) + g.shape[3:])


def _mla(cq, ckv, kr, pos, kv_past, g_q, w_uq, g_kv, w_uk, w_uv):
    q = jnp.einsum('btc,chd->bthd', _rms(cq, g_q), w_uq)
    q_nope, q_rope = q[..., :MLA_NOPE], _rope(q[..., MLA_NOPE:], pos)
    kv_new = jnp.concatenate([_rms(ckv, g_kv), _rope(kr, pos)], axis=-1)
    kv_all = kv_new if kv_past is None else jnp.concatenate([kv_past, kv_new], axis=1)
    lat = kv_all[..., :MLA_KV_LORA]
    k_pos = jnp.arange(kv_all.shape[1])
    q_cat = jnp.concatenate([jnp.einsum('bthd,rhd->bthr', q_nope, w_uk), q_rope], axis=-1)
    scale = (MLA_NOPE + MLA_ROPE) ** -0.5

    def block(s0, qb):
        qc = lax.dynamic_slice_in_dim(q_cat, s0, qb, axis=1)
        qp = lax.dynamic_slice_in_dim(pos, s0, qb)
        s = jnp.einsum('bqhc,bkc->bhqk', qc, kv_all).astype(jnp.float32) * scale
        p = _masked_softmax(s, k_pos[None, :] <= qp[:, None])
        return jnp.einsum('bhqk,bkr->bqhr', p.astype(lat.dtype), lat)

    o_lat = _sweep(block, q_cat.shape[1])
    o = jnp.einsum('bthr,rhv->bthv', o_lat, w_uv)
    return o.reshape(o.shape[:2] + (-1,)), kv_new


def _diff(q, k, v, pos, kv_past, lam_p, g_sub, lam_init):
    B, T = q.shape[:2]
    G, R = DIFF_KV_HEADS, DIFF_HEADS // DIFF_KV_HEADS
    q = q.reshape(B, T, G, R, 2, DIFF_D)
    kv_new = jnp.stack([k.reshape(B, T, G, 2 * DIFF_D), v.reshape(B, T, G, 2 * DIFF_D)], axis=2)
    kv_all = kv_new if kv_past is None else jnp.concatenate([kv_past, kv_new], axis=1)
    Tk = kv_all.shape[1]
    k_all = kv_all[:, :, 0].reshape(B, Tk, G, 2, DIFF_D)
    v_all = kv_all[:, :, 1]
    k_pos = jnp.arange(Tk)
    lp = lam_p.astype(jnp.float32)
    lam = jnp.exp(jnp.sum(lp[0] * lp[1])) - jnp.exp(jnp.sum(lp[2] * lp[3])) + lam_init
    slopes = _alibi_slopes(DIFF_HEADS).reshape(1, G, R, 1, 1, 1)

    def block(s0, qb):
        qc = lax.dynamic_slice_in_dim(q, s0, qb, axis=1)
        qp = lax.dynamic_slice_in_dim(pos, s0, qb)
        dist = qp[:, None] - k_pos[None, :]
        s = jnp.einsum('bqgrcd,bkgcd->bgrcqk', qc, k_all).astype(jnp.float32) * (DIFF_D ** -0.5)
        p = _masked_softmax(s - slopes * dist.astype(jnp.float32), dist >= 0)
        a = p[:, :, :, 0] - lam * p[:, :, :, 1]
        return jnp.einsum('bgrqk,bkge->bqgre', a.astype(v_all.dtype), v_all)

    o = _sweep(block, T)
    o = _rms(o, g_sub) * (1.0 - lam_init)
    return o.reshape(B, T, -1), kv_new


def _compress(x, pe, w1, b1, w2):
    B, S, G, dk = x.shape
    nc = S // CMP_STRIDE
    xc = x[:, :nc * CMP_STRIDE].reshape(B, nc, CMP_STRIDE, G, dk)
    lo = jnp.einsum('bnjgd,jdh->bngh', xc + pe[:CMP_STRIDE, None, :], w1[:CMP_STRIDE])
    hi = jnp.einsum('bnjgd,jdh->bngh', xc + pe[CMP_STRIDE:, None, :], w1[CMP_STRIDE:])
    h = jax.nn.silu(lo[:, :-1] + hi[:, 1:] + b1)
    return jnp.einsum('bngh,hd->bngd', h, w2)


def _nsa(q, rows_new, win_new, gate_logits, pos, pos0, rows_past, win_past, pe, w1, b1, w2):
    B, T = q.shape[:2]
    G, R = NSA_GROUPS, NSA_HEADS // NSA_GROUPS
    q = q.reshape(B, T, G, R, NSA_DK)
    gates = jax.nn.sigmoid(gate_logits.astype(jnp.float32)).reshape(B, T, G, R, 3)
    rows = rows_new if rows_past is None else jnp.concatenate([rows_past, rows_new], axis=1)
    S = rows.shape[1]
    if win_past is None:
        win = jnp.pad(win_new, ((0, 0), (WINDOW, 0), (0, 0), (0, 0), (0, 0)))
        win_pos0 = -WINDOW
        win_state = win_new[:, T - min(WINDOW, T):]
    else:
        win = jnp.concatenate([win_past, win_new], axis=1)
        win_pos0 = pos0 - win_past.shape[1]
        win_state = win[:, T:]
    span = win.shape[1] - T
    k_cmp = _compress(rows[:, :, 0], pe[0], w1[0], b1[0], w2[0])
    v_cmp = _compress(rows[:, :, 1], pe[1], w1[1], b1[1], w2[1])
    n_cmp = k_cmp.shape[1]
    cmp_start = jnp.arange(n_cmp) * CMP_STRIDE
    cmp_end = cmp_start + CMP_BLOCK - 1
    n_sel = -(-S // SEL_BLOCK)
    sel_start = jnp.arange(n_sel) * SEL_BLOCK
    overlap = ((cmp_start[:, None] < sel_start[None, :] + SEL_BLOCK)
               & (cmp_end[:, None] >= sel_start[None, :])).astype(jnp.float32)
    sel = jnp.pad(rows[:, :, 2:], ((0, 0), (0, n_sel * SEL_BLOCK - S), (0, 0), (0, 0), (0, 0)))
    sel = sel.reshape(B, n_sel, SEL_BLOCK, 2, G, NSA_DK).transpose(0, 4, 3, 1, 2, 5)
    n_top = min(SEL_TOP, n_sel)
    slopes = _alibi_slopes(NSA_HEADS).reshape(G, R)
    scale = NSA_DK ** -0.5
    take_blocks = jax.vmap(jax.vmap(lambda blk, ix: blk[:, ix]))
    j_sel = jnp.arange(n_sel)

    def block(s0, qb):
        qc = lax.dynamic_slice_in_dim(q, s0, qb, axis=1)
        qp = lax.dynamic_slice_in_dim(pos, s0, qb)
        gc = lax.dynamic_slice_in_dim(gates, s0, qb, axis=1)
        dist_c = qp[:, None] - cmp_end[None, :]
        s = jnp.einsum('bqgrd,bngd->bgrqn', qc, k_cmp).astype(jnp.float32) * scale
        p_c = _masked_softmax(s - slopes[None, :, :, None, None] * dist_c.astype(jnp.float32), dist_c >= 0)
        o_c = jnp.einsum('bgrqn,bngd->bqgrd', p_c.astype(v_cmp.dtype), v_cmp)
        imp = jnp.einsum('bgrqn,ns->bgqs', p_c, overlap)
        cur = qp // SEL_BLOCK
        valid = sel_start[None, :] <= qp[:, None]
        forced = (j_sel[None, :] == 0) | (j_sel[None, :] == cur[:, None]) | (j_sel[None, :] == cur[:, None] - 1)
        score = jnp.where(valid, jnp.where(forced, jnp.inf, imp), -jnp.inf)
        _, idx = lax.top_k(score, n_top)
        kv_sel = take_blocks(sel, idx)
        kpos = idx[..., None] * SEL_BLOCK + jnp.arange(SEL_BLOCK)
        dist_s = qp[None, None, :, None, None] - kpos
        s = jnp.einsum('bqgrd,bgqnld->bgrqnl', qc, kv_sel[:, :, 0]).astype(jnp.float32) * scale
        s = s - slopes[None, :, :, None, None, None] * dist_s[:, :, None].astype(jnp.float32)
        sh = s.shape
        p_s = _masked_softmax(s.reshape(sh[:4] + (-1,)),
                              (dist_s >= 0)[:, :, None].reshape(sh[0], sh[1], 1, sh[3], -1)).reshape(sh)
        o_s = jnp.einsum('bgrqnl,bgqnld->bqgrd', p_s.astype(kv_sel.dtype), kv_sel[:, :, 1])
        w = lax.dynamic_slice_in_dim(win, s0, span + qb, axis=1)
        wpos = win_pos0 + s0 + jnp.arange(span + qb)
        dist_w = qp[:, None] - wpos[None, :]
        mask_w = (wpos[None, :] >= 0) & (dist_w >= 0) & (dist_w <= WINDOW)
        s = jnp.einsum('bqgrd,bkgd->bgrqk', qc, w[:, :, 0]).astype(jnp.float32) * scale
        p_w = _masked_softmax(s - slopes[None, :, :, None, None] * dist_w.astype(jnp.float32), mask_w)
        o_w = jnp.einsum('bgrqk,bkgd->bqgrd', p_w.astype(w.dtype), w[:, :, 1])
        return (gc[..., 0:1] * o_c + gc[..., 1:2] * o_s + gc[..., 2:3] * o_w).astype(q.dtype)

    o = _sweep(block, T, SEL_Q_BLOCK)
    return o.reshape(B, T, -1), win_state


def _pad_cols(w, n):
    return jnp.pad(w, ((0, 0), (0, n - w.shape[1])))


def _run(x, c, p, cache_mla=None, cache_diff=None, cache_nsa=None, state_win=None, page_table=None):
    B, T, _ = x.shape
    has_past = page_table is not None
    pos0 = page_table.shape[1] * PAGE_SIZE if has_past else 0
    pos = pos0 + jnp.arange(T)
    mod = jnp.einsum('bc,lcd->lbd', jax.nn.silu(c), p['ada_w']) + p['ada_b'][:, None, :]
    mla_rows, diff_rows, nsa_rows, win_rows = [], [], [], []
    h = x
    n_ab, n_c = sum(AB_SPLIT), sum(C_SPLIT)
    for l in range(DEPTH):
        shift, scale, gate = jnp.split(mod[l][:, None, :], 3, axis=-1)
        pre_g = p['pre_g'][l].reshape(1, 1, D_MODEL)
        post_g = p['post_g'][l].reshape(1, 1, D_MODEL)
        i = l // 2
        if l % 2 == 0:
            w_in = _pad_cols(p['ab_w_in'][i], _round_up(n_ab, LANE)).astype(jnp.bfloat16)
            proj = _inproj(h, scale, shift, pre_g, w_in)[..., :n_ab]
            cq, ckv, kr, ga, qd, kd, vd, gb = _split(proj, AB_SPLIT)
            o_a, kv_a = _mla(cq, ckv, kr, pos,
                             _gather_pages(cache_mla, page_table, i) if has_past else None,
                             p['mla_q_g'][i], p['mla_w_uq'][i], p['mla_kv_g'][i], p['mla_w_uk'][i], p['mla_w_uv'][i])
            o_b, kv_b = _diff(qd, kd, vd, pos,
                              _gather_pages(cache_diff, page_table, i) if has_past else None,
                              p['diff_lam'][i], p['diff_sub_g'][i], 0.8 - 0.6 * math.exp(-0.3 * l))
            o = jnp.concatenate([o_a, o_b], axis=-1)
            gsrc = jnp.concatenate([ga, gb], axis=-1)
            h = _outproj(o, gsrc, h, gate, post_g, p['ab_w_out'][i].astype(jnp.bfloat16))
            mla_rows.append(kv_a)
            diff_rows.append(kv_b)
        else:
            w_in = _pad_cols(p['c_w_in'][i], _round_up(n_c, LANE)).astype(jnp.bfloat16)
            proj = _inproj(h, scale, shift, pre_g, w_in)[..., :n_c]
            qn, kc, vc, ks, vs, kw, vw, gl, gn = _split(proj, C_SPLIT)
            grp = lambda t: t.reshape(B, T, NSA_GROUPS, NSA_DK)
            rows_new = jnp.stack([grp(kc), grp(vc), grp(ks), grp(vs)], axis=2)
            win_new = jnp.stack([grp(kw), grp(vw)], axis=2)
            if has_past:
                o_n, win_state = _nsa(qn, rows_new, win_new, gl, pos, pos0,
                                      _gather_pages(cache_nsa, page_table, i), state_win[:, i],
                                      p['nsa_pe'][i], p['nsa_w1'][i], p['nsa_b1'][i], p['nsa_w2'][i])
            else:
                pe, w1, b1, w2 = p['nsa_pe'][i], p['nsa_w1'][i], p['nsa_b1'][i], p['nsa_w2'][i]
                k_cmp = _compress(grp(kc), pe[0], w1[0], b1[0], w2[0])
                v_cmp = _compress(grp(vc), pe[1], w1[1], b1[1], w2[1])
                o_n = _nsa_prompt(qn, k_cmp, v_cmp, ks, vs, kw, vw, gl)
                win_state = win_new[:, T - min(WINDOW, T):]
            h = _outproj(o_n, gn, h, gate, post_g, p['c_w_out'][i].astype(jnp.bfloat16))
            nsa_rows.append(rows_new)
            win_rows.append(win_state)
    return (h, jnp.stack(mla_rows, axis=2), jnp.stack(diff_rows, axis=2),
            jnp.stack(nsa_rows, axis=2), jnp.stack(win_rows, axis=1))


def kernel(x_prompt, x_sample, c_prompt, c_sample, cache_mla, cache_diff, cache_nsa, state_nsa_win, page_table,
           ada_w, ada_b, pre_g, post_g, ab_w_in, mla_q_g, mla_w_uq, mla_kv_g, mla_w_uk, mla_w_uv,
           diff_lam, diff_sub_g, ab_w_out, c_w_in, nsa_pe, nsa_w1, nsa_b1, nsa_w2, c_w_out):
    p = dict(ada_w=ada_w, ada_b=ada_b, pre_g=pre_g, post_g=post_g, ab_w_in=ab_w_in,
             mla_q_g=mla_q_g, mla_w_uq=mla_w_uq, mla_kv_g=mla_kv_g, mla_w_uk=mla_w_uk, mla_w_uv=mla_w_uv,
             diff_lam=diff_lam, diff_sub_g=diff_sub_g, ab_w_out=ab_w_out, c_w_in=c_w_in,
             nsa_pe=nsa_pe, nsa_w1=nsa_w1, nsa_b1=nsa_b1, nsa_w2=nsa_w2, c_w_out=c_w_out)
    y_prompt, mla_p, diff_p, nsa_p, win_p = _run(x_prompt, c_prompt, p)
    y_sample, mla_s, diff_s, nsa_s, win_s = _run(x_sample, c_sample, p, cache_mla, cache_diff, cache_nsa,
                                                 state_nsa_win, page_table)
    return (y_prompt, y_sample, mla_p, mla_s, diff_p, diff_s, nsa_p, nsa_s, win_p, win_s)
```

```python
import math
from functools import partial

import jax
import jax.numpy as jnp
import numpy as np
from jax import lax
from jax.experimental import pallas as pl
from jax.experimental.pallas import tpu as pltpu

D_MODEL = 1024
DEPTH = 4
PAGE_SIZE = 128
MLA_HEADS = 8
MLA_NOPE = 64
MLA_ROPE = 32
MLA_V = 64
MLA_Q_LORA = 768
MLA_KV_LORA = 256
ROPE_THETA = 10000.0
DIFF_HEADS = 4
DIFF_KV_HEADS = 2
DIFF_D = 64
NSA_HEADS = 16
NSA_GROUPS = 2
NSA_DK = 64
NSA_DV = 64
CMP_BLOCK = 32
CMP_STRIDE = 16
CMP_HIDDEN = 128
SEL_BLOCK = 64
SEL_TOP = 16
WINDOW = 512
Q_BLOCK = 128
SEL_Q_BLOCK = 32
EPS = 1e-6

AB_SPLIT = (MLA_Q_LORA, MLA_KV_LORA, MLA_ROPE, MLA_HEADS * MLA_V,
            DIFF_HEADS * 2 * DIFF_D, DIFF_KV_HEADS * 2 * DIFF_D, DIFF_KV_HEADS * 2 * DIFF_D, DIFF_HEADS * 2 * DIFF_D)
C_SPLIT = (NSA_HEADS * NSA_DK,) + (NSA_GROUPS * NSA_DK,) * 6 + (3 * NSA_HEADS, NSA_HEADS * NSA_DV)

LANE = 128
ROW_TILE = 512
VMEM_LIMIT = 56 * 1024 * 1024


def _round_up(n, m):
    return -(-n // m) * m


def _inproj_body(h_ref, scale_ref, shift_ref, g_ref, w_ref, o_ref):
    bb, tt, d = h_ref.shape
    h = h_ref[...]
    y = h * lax.rsqrt(jnp.mean(h * h, axis=-1, keepdims=True) + EPS)
    u = (y * g_ref[...]) * (1.0 + scale_ref[...]) + shift_ref[...]
    u = u.reshape(bb * tt, d).astype(jnp.bfloat16)
    o = jnp.dot(u, w_ref[...], preferred_element_type=jnp.float32)
    o_ref[...] = o.reshape(bb, tt, o.shape[-1])


def _row_blocks(B, T):
    tt = min(T, ROW_TILE)
    bb = max(1, ROW_TILE // tt)
    bb = min(bb, B)
    assert T % tt == 0 and B % bb == 0
    return bb, tt


def _inproj(h, scale, shift, g, w_bf16):
    B, T, D = h.shape
    N = w_bf16.shape[1]
    bb, tt = _row_blocks(B, T)
    return pl.pallas_call(
        _inproj_body,
        grid=(B // bb, T // tt),
        in_specs=[
            pl.BlockSpec((bb, tt, D), lambda i, j: (i, j, 0)),
            pl.BlockSpec((bb, 1, D), lambda i, j: (i, 0, 0)),
            pl.BlockSpec((bb, 1, D), lambda i, j: (i, 0, 0)),
            pl.BlockSpec((1, 1, D), lambda i, j: (0, 0, 0)),
            pl.BlockSpec((D, N), lambda i, j: (0, 0)),
        ],
        out_specs=pl.BlockSpec((bb, tt, N), lambda i, j: (i, j, 0)),
        out_shape=jax.ShapeDtypeStruct((B, T, N), jnp.float32),
        compiler_params=pltpu.CompilerParams(
            dimension_semantics=("arbitrary", "arbitrary"), vmem_limit_bytes=VMEM_LIMIT),
        name="inproj",
    )(h, scale, shift, g, w_bf16)


def _outproj_body(o_ref, gsrc_ref, h_ref, gate_ref, pg_ref, w_ref, out_ref):
    bb, tt, d = h_ref.shape
    gs = gsrc_ref[...]
    a = o_ref[...] * (gs * jax.nn.sigmoid(gs))
    a = a.reshape(bb * tt, a.shape[-1]).astype(jnp.bfloat16)
    y = jnp.dot(a, w_ref[...], preferred_element_type=jnp.float32)
    y = y * lax.rsqrt(jnp.mean(y * y, axis=-1, keepdims=True) + EPS)
    y = y.reshape(bb, tt, d) * pg_ref[...]
    out_ref[...] = h_ref[...] + gate_ref[...] * y


def _outproj(o, gsrc, h, gate, pg, w_bf16):
    B, T, D = h.shape
    W = o.shape[-1]
    bb, tt = _row_blocks(B, T)
    return pl.pallas_call(
        _outproj_body,
        grid=(B // bb, T // tt),
        in_specs=[
            pl.BlockSpec((bb, tt, W), lambda i, j: (i, j, 0)),
            pl.BlockSpec((bb, tt, W), lambda i, j: (i, j, 0)),
            pl.BlockSpec((bb, tt, D), lambda i, j: (i, j, 0)),
            pl.BlockSpec((bb, 1, D), lambda i, j: (i, 0, 0)),
            pl.BlockSpec((1, 1, D), lambda i, j: (0, 0, 0)),
            pl.BlockSpec((W, D), lambda i, j: (0, 0)),
        ],
        out_specs=pl.BlockSpec((bb, tt, D), lambda i, j: (i, j, 0)),
        out_shape=jax.ShapeDtypeStruct((B, T, D), jnp.float32),
        compiler_params=pltpu.CompilerParams(
            dimension_semantics=("arbitrary", "arbitrary"), vmem_limit_bytes=VMEM_LIMIT),
        name="outproj",
    )(o, gsrc, h, gate, pg, w_bf16)


NSA_TQ = 128
NSA_TK = 256
NEG_INF = float("-inf")


def _dot_nt(a, b):
    return lax.dot_general(a, b, (((1,), (1,)), ((), ())), preferred_element_type=jnp.float32)


def _dot_split3(a, b01):
    hi = a.astype(jnp.bfloat16)
    r1 = a - hi.astype(jnp.float32)
    mid = r1.astype(jnp.bfloat16)
    lo = (r1 - mid.astype(jnp.float32)).astype(jnp.bfloat16)
    dot = partial(jnp.dot, preferred_element_type=jnp.float32)
    return dot(hi, b01) + dot(mid, b01) + dot(lo, b01)


def _flash_step(q, k, v, slope3, dist, mask, carry, scale=None):
    m, l, acc = carry
    R, TQ, _ = m.shape
    s = _dot_nt(q, k.astype(jnp.bfloat16))
    s3 = s.reshape(R, TQ, s.shape[-1])
    if scale is not None:
        s3 = s3 * scale
    if slope3 is not None:
        s3 = s3 - slope3 * dist[None]
    s3 = jnp.where(mask[None], s3, NEG_INF)
    m_new = jnp.maximum(m, jnp.max(s3, axis=-1, keepdims=True))
    m_safe = jnp.where(m_new == NEG_INF, 0.0, m_new)
    p = jnp.exp(s3 - m_safe)
    alpha = jnp.exp(m - m_safe)
    l = alpha * l + jnp.sum(p, axis=-1, keepdims=True)
    pv = jnp.dot(p.reshape(R * TQ, p.shape[-1]).astype(jnp.bfloat16), v.astype(jnp.bfloat16),
                 preferred_element_type=jnp.float32)
    acc = alpha * acc + pv.reshape(R, TQ, pv.shape[-1])
    return m_new, l, acc


def _flash_init(R, TQ, dv):
    return (jnp.full((R, TQ, 1), NEG_INF, jnp.float32), jnp.zeros((R, TQ, 1), jnp.float32),
            jnp.zeros((R, TQ, dv), jnp.float32))


def _flash_out(carry):
    _, l, acc = carry
    return acc / jnp.where(l > 0, l, 1.0)


def _select_blocks(imp, qpos, n_top, n_real=None):
    TQ, n_sel = imp.shape
    n_real = n_sel if n_real is None else n_real
    j = lax.broadcasted_iota(jnp.int32, (TQ, n_sel), 1)
    cur = qpos // SEL_BLOCK
    valid = j * SEL_BLOCK <= qpos
    forced = (j == 0) | (j == cur) | (j == cur - 1)
    score = jnp.where(valid, jnp.where(forced, jnp.inf, imp), NEG_INF)
    rank = jnp.zeros((TQ, n_sel), jnp.int32)
    for k in range(n_real):
        col = score[:, k:k + 1]
        beats = (col > score) | ((col == score) & (j > k))
        rank = rank + beats.astype(jnp.int32)
    return (rank < n_top).astype(jnp.float32)


def _nsa_prompt_body(q_ref, kc_ref, vc_ref, ks_ref, vs_ref, kw_ref, vw_ref, gl_ref, slope_ref, ov_ref, et_ref,
                     o_ref, *, n_top):
    _, _, R, TQ, dk = q_ref.shape
    TK = NSA_TK
    qt = pl.program_id(2)
    s0 = qt * TQ
    q = (q_ref[0, 0].reshape(R * TQ, dk) * (NSA_DK ** -0.5)).astype(jnp.bfloat16)
    slope3 = slope_ref[0]
    qpos = s0 + lax.broadcasted_iota(jnp.int32, (TQ, 1), 0)

    n_cmp = kc_ref.shape[2]
    cmp_end = lax.broadcasted_iota(jnp.int32, (1, n_cmp), 1) * CMP_STRIDE + (CMP_BLOCK - 1)
    dist_c = qpos - cmp_end
    s = _dot_nt(q, kc_ref[0, 0].astype(jnp.bfloat16))
    s3 = s.reshape(R, TQ, n_cmp) - slope3 * dist_c.astype(jnp.float32)[None]
    s3 = jnp.where((dist_c >= 0)[None], s3, NEG_INF)
    m_c = jnp.max(s3, axis=-1, keepdims=True)
    e_c = jnp.exp(s3 - jnp.where(m_c == NEG_INF, 0.0, m_c))
    d_c = jnp.sum(e_c, axis=-1, keepdims=True)
    p_c = e_c / jnp.where(d_c > 0, d_c, 1.0)
    o_c = jnp.dot(p_c.reshape(R * TQ, n_cmp).astype(jnp.bfloat16), vc_ref[0, 0].astype(jnp.bfloat16),
                  preferred_element_type=jnp.float32).reshape(R, TQ, dk)
    imp = _dot_split3(jnp.sum(p_c, axis=0), ov_ref[...])
    selmask = _select_blocks(imp, qpos, n_top).astype(jnp.bfloat16)

    def sel_step(kt, carry):
        k0 = pl.multiple_of(kt * TK, TK)
        kpos = k0 + lax.broadcasted_iota(jnp.int32, (1, TK), 1)
        dist = qpos - kpos
        picked = _dot_nt(selmask, et_ref[pl.ds(k0, TK), :]) > 0.5
        return _flash_step(q, ks_ref[0, 0, pl.ds(k0, TK), :], vs_ref[0, 0, pl.ds(k0, TK), :], slope3,
                           dist.astype(jnp.float32), picked & (dist >= 0), carry)

    n_kt = (s0 + TQ + TK - 1) // TK
    o_s = _flash_out(lax.fori_loop(0, n_kt, sel_step, _flash_init(R, TQ, dk)))

    def win_step(kt, carry):
        k0 = pl.multiple_of(kt * TK, TK)
        kpos = k0 + lax.broadcasted_iota(jnp.int32, (1, TK), 1)
        dist = qpos - kpos
        return _flash_step(q, kw_ref[0, 0, pl.ds(k0, TK), :], vw_ref[0, 0, pl.ds(k0, TK), :], slope3,
                           dist.astype(jnp.float32), (dist >= 0) & (dist <= WINDOW), carry)

    kt_lo = jnp.maximum(s0 - WINDOW, 0) // TK
    o_w = _flash_out(lax.fori_loop(kt_lo, n_kt, win_step, _flash_init(R, TQ, dk)))

    gates = jax.nn.sigmoid(gl_ref[0, 0])
    o_ref[0, 0] = gates[..., 0:1] * o_c + gates[..., 1:2] * o_s + gates[..., 2:3] * o_w


def _nsa_prompt(q, k_cmp, v_cmp, ks, vs, kw, vw, gl):
    B, T, _ = q.shape
    G, R, dk = NSA_GROUPS, NSA_HEADS // NSA_GROUPS, NSA_DK
    TQ, TK = NSA_TQ, NSA_TK
    assert T % TK == 0 and TK % TQ == 0 and TK % SEL_BLOCK == 0
    n_sel = T // SEL_BLOCK
    n_top = min(SEL_TOP, n_sel)
    n_cmp = k_cmp.shape[1]
    n_cmp_pad = _round_up(n_cmp, LANE)
    q5 = q.reshape(B, T, G, R, dk).transpose(0, 2, 3, 1, 4)
    gl5 = gl.reshape(B, T, G, R, 3).transpose(0, 2, 3, 1, 4)
    by_group = lambda t: t.reshape(B, -1, G, dk).transpose(0, 2, 1, 3)
    pad_cmp = lambda t: jnp.pad(by_group(t.reshape(B, n_cmp, G * dk)), ((0, 0), (0, 0), (0, n_cmp_pad - n_cmp), (0, 0)))
    slopes = _alibi_slopes(NSA_HEADS).reshape(G, R, 1, 1)
    cmp_start = np.arange(n_cmp_pad) * CMP_STRIDE
    sel_start = np.arange(n_sel) * SEL_BLOCK
    overlap = ((cmp_start[:, None] < sel_start[None, :] + SEL_BLOCK)
               & (cmp_start[:, None] + CMP_BLOCK - 1 >= sel_start[None, :]) & (np.arange(n_cmp_pad)[:, None] < n_cmp))
    expand_t = (np.arange(T)[:, None] // SEL_BLOCK) == np.arange(n_sel)[None, :]
    row_spec = lambda rows: pl.BlockSpec((1, 1, rows, dk), lambda b, g, t: (b, g, 0, 0))
    out = pl.pallas_call(
        partial(_nsa_prompt_body, n_top=n_top),
        grid=(B, G, T // TQ),
        in_specs=[
            pl.BlockSpec((1, 1, R, TQ, dk), lambda b, g, t: (b, g, 0, t, 0)),
            row_spec(n_cmp_pad), row_spec(n_cmp_pad), row_spec(T), row_spec(T), row_spec(T), row_spec(T),
            pl.BlockSpec((1, 1, R, TQ, 3), lambda b, g, t: (b, g, 0, t, 0)),
            pl.BlockSpec((1, R, 1, 1), lambda b, g, t: (g, 0, 0, 0)),
            pl.BlockSpec((n_cmp_pad, n_sel), lambda b, g, t: (0, 0)),
            pl.BlockSpec((T, n_sel), lambda b, g, t: (0, 0)),
        ],
        out_specs=pl.BlockSpec((1, 1, R, TQ, dk), lambda b, g, t: (b, g, 0, t, 0)),
        out_shape=jax.ShapeDtypeStruct((B, G, R, T, dk), jnp.float32),
        compiler_params=pltpu.CompilerParams(
            dimension_semantics=("arbitrary", "arbitrary", "arbitrary"), vmem_limit_bytes=VMEM_LIMIT),
        name="nsa_prompt",
    )(q5, pad_cmp(k_cmp), pad_cmp(v_cmp), by_group(ks), by_group(vs), by_group(kw), by_group(vw), gl5, slopes,
      jnp.asarray(overlap, jnp.bfloat16), jnp.asarray(expand_t, jnp.bfloat16))
    return out.transpose(0, 3, 1, 2, 4).reshape(B, T, G * R * dk)


FLASH_TK = 256


def _causal_sweep(q, k_ref, v_of, slope3, scale, s0, R, TQ, dv):
    TK = FLASH_TK
    qpos = s0 + lax.broadcasted_iota(jnp.int32, (TQ, 1), 0)

    def step(kt, carry):
        k0 = pl.multiple_of(kt * TK, TK)
        dist = qpos - (k0 + lax.broadcasted_iota(jnp.int32, (1, TK), 1))
        k = k_ref[pl.ds(k0, TK), :]
        return _flash_step(q, k, v_of(k, k0), slope3, dist.astype(jnp.float32), dist >= 0, carry, scale)

    return lax.fori_loop(0, (s0 + TQ + TK - 1) // TK, step, _flash_init(R, TQ, dv))


def _mla_prompt_body(q_ref, kv_ref, o_ref):
    _, H, TQ, C = q_ref.shape
    s0 = pl.program_id(1) * TQ
    q = q_ref[0].reshape(H * TQ, C).astype(jnp.bfloat16)
    carry = _causal_sweep(q, kv_ref.at[0], lambda k, k0: k[:, :MLA_KV_LORA], None,
                          (MLA_NOPE + MLA_ROPE) ** -0.5, s0, H, TQ, MLA_KV_LORA)
    o_ref[0] = _flash_out(carry)


def _mla_prompt(q_cat, kv_new):
    B, T, H, C = q_cat.shape
    TQ = 128
    assert T % FLASH_TK == 0 and FLASH_TK % TQ == 0
    o = pl.pallas_call(
        _mla_prompt_body,
        grid=(B, T // TQ),
        in_specs=[pl.BlockSpec((1, H, TQ, C), lambda b, t: (b, 0, t, 0)),
                  pl.BlockSpec((1, T, C), lambda b, t: (b, 0, 0))],
        out_specs=pl.BlockSpec((1, H, TQ, MLA_KV_LORA), lambda b, t: (b, 0, t, 0)),
        out_shape=jax.ShapeDtypeStruct((B, H, T, MLA_KV_LORA), jnp.float32),
        compiler_params=pltpu.CompilerParams(
            dimension_semantics=("arbitrary", "arbitrary"), vmem_limit_bytes=VMEM_LIMIT),
        name="mla_prompt",
    )(q_cat.transpose(0, 2, 1, 3), kv_new)
    return o.transpose(0, 2, 1, 3)


def _diff_prompt_body(q_ref, k_ref, v_ref, slope_ref, lam_ref, gsub_ref, o_ref, *, out_scale):
    _, _, RC, TQ, d2 = q_ref.shape
    s0 = pl.program_id(2) * TQ
    q = q_ref[0, 0].reshape(RC * TQ, d2).astype(jnp.bfloat16)
    carry = _causal_sweep(q, k_ref.at[0, 0], lambda k, k0: v_ref[0, 0, pl.ds(k0, FLASH_TK), :], slope_ref[0],
                          DIFF_D ** -0.5, s0, RC, TQ, d2)
    pn = _flash_out(carry).reshape(RC // 2, 2, TQ, d2)
    o = pn[:, 0] - lam_ref[...] * pn[:, 1]
    o = o * lax.rsqrt(jnp.mean(o * o, axis=-1, keepdims=True) + EPS)
    o_ref[0, 0] = o * gsub_ref[...] * out_scale


def _diff_prompt(q, k, v, lam, g_sub, lam_init):
    B, T, _ = q.shape
    G, R, d2 = DIFF_KV_HEADS, DIFF_HEADS // DIFF_KV_HEADS, 2 * DIFF_D
    TQ = 256
    assert T % FLASH_TK == 0 and T % TQ == 0
    q6 = q.reshape(B, T, G, R, 2, DIFF_D).transpose(0, 2, 3, 4, 1, 5)
    zero = jnp.zeros_like(q6[:, :, :, 0])
    q_pad = jnp.stack([jnp.concatenate([q6[:, :, :, 0], zero], axis=-1),
                       jnp.concatenate([zero, q6[:, :, :, 1]], axis=-1)], axis=3)
    q_pad = q_pad.reshape(B, G, R * 2, T, d2)
    by_group = lambda t: t.reshape(B, T, G, d2).transpose(0, 2, 1, 3)
    slopes = jnp.repeat(_alibi_slopes(DIFF_HEADS).reshape(G, R), 2, axis=1).reshape(G, R * 2, 1, 1)
    o = pl.pallas_call(
        partial(_diff_prompt_body, out_scale=1.0 - lam_init),
        grid=(B, G, T // TQ),
        in_specs=[pl.BlockSpec((1, 1, R * 2, TQ, d2), lambda b, g, t: (b, g, 0, t, 0)),
                  pl.BlockSpec((1, 1, T, d2), lambda b, g, t: (b, g, 0, 0)),
                  pl.BlockSpec((1, 1, T, d2), lambda b, g, t: (b, g, 0, 0)),
                  pl.BlockSpec((1, R * 2, 1, 1), lambda b, g, t: (g, 0, 0, 0)),
                  pl.BlockSpec((1, 1), lambda b, g, t: (0, 0)),
                  pl.BlockSpec((1, d2), lambda b, g, t: (0, 0))],
        out_specs=pl.BlockSpec((1, 1, R, TQ, d2), lambda b, g, t: (b, g, 0, t, 0)),
        out_shape=jax.ShapeDtypeStruct((B, G, R, T, d2), jnp.float32),
        compiler_params=pltpu.CompilerParams(
            dimension_semantics=("arbitrary", "arbitrary", "arbitrary"), vmem_limit_bytes=VMEM_LIMIT),
        name="diff_prompt",
    )(q_pad, by_group(k), by_group(v), slopes, lam.reshape(1, 1), g_sub.reshape(1, d2))
    return o.transpose(0, 3, 1, 2, 4).reshape(B, T, G * R * d2)


PAGES_PER_STEP = 8


def _page_specs(block, n_pages, tail):
    P = PAGES_PER_STEP
    tail = tuple(tail) + (0,) * (len(block) - 1 - len(tail))

    def index(b, j, pt_ref, *, k):
        return (pt_ref[b * n_pages + j * P + k],) + tail

    return [pl.BlockSpec(block, partial(index, k=k)) for k in range(P)]


def _softmax_update(s, v_list, m_ref, l_ref, acc_ref, vt=False):
    m_prev = m_ref[...]
    m_new = jnp.maximum(m_prev, jnp.max(s, axis=-1, keepdims=True))
    m_safe = jnp.where(m_new == NEG_INF, 0.0, m_new)
    p = jnp.exp(s - m_safe)
    alpha = jnp.exp(m_prev - m_safe)
    l_ref[...] = alpha * l_ref[...] + jnp.sum(p, axis=-1, keepdims=True)
    pv = None
    off = 0
    for v in v_list:
        n = v.shape[1] if vt else v.shape[0]
        pc = p[:, off:off + n].astype(jnp.bfloat16)
        term = (_dot_nt(pc, v.astype(jnp.bfloat16)) if vt
                else jnp.dot(pc, v.astype(jnp.bfloat16), preferred_element_type=jnp.float32))
        pv = term if pv is None else pv + term
        off += n
    acc_ref[...] = alpha * acc_ref[...] + pv
    m_ref[...] = m_new


def _mla_decode_body(pt_ref, q_ref, kvn_ref, *rest, n_tok):
    P = PAGES_PER_STEP
    pages, (o_ref, m_ref, l_ref, acc_ref) = rest[:P], rest[P:]
    j = pl.program_id(1)
    scale = (MLA_NOPE + MLA_ROPE) ** -0.5
    q = q_ref[0].astype(jnp.bfloat16)

    @pl.when(j == 0)
    def _():
        m_ref[...] = jnp.full(m_ref.shape, NEG_INF, jnp.float32)
        l_ref[...] = jnp.zeros(l_ref.shape, jnp.float32)
        acc_ref[...] = jnp.zeros(acc_ref.shape, jnp.float32)

    kvs = [pg[...] for pg in pages]
    s = jnp.concatenate([jnp.dot(q, kv.astype(jnp.bfloat16), preferred_element_type=jnp.float32) for kv in kvs],
                        axis=-1) * scale
    _softmax_update(s, [kv[:MLA_KV_LORA] for kv in kvs], m_ref, l_ref, acc_ref, vt=True)

    @pl.when(j == pl.num_programs(1) - 1)
    def _():
        kvn = kvn_ref[0]
        sn = _dot_nt(q, kvn.astype(jnp.bfloat16)) * scale
        tq = lax.broadcasted_iota(jnp.int32, sn.shape, 0) % n_tok
        tk = lax.broadcasted_iota(jnp.int32, sn.shape, 1)
        _softmax_update(jnp.where(tk <= tq, sn, NEG_INF), [kvn[:, :MLA_KV_LORA]], m_ref, l_ref, acc_ref)
        l = l_ref[...]
        o_ref[0] = acc_ref[...] / jnp.where(l > 0, l, 1.0)


def _mla_decode(q_cat, kv_new, cache_mla, page_table, layer):
    B, T, H, C = q_cat.shape
    n_pages = page_table.shape[1]
    P = PAGES_PER_STEP
    assert n_pages % P == 0
    q_rows = q_cat.transpose(0, 2, 1, 3).reshape(B, H * T, C)
    cache_t = cache_mla.transpose(0, 2, 3, 1)
    grid_spec = pltpu.PrefetchScalarGridSpec(
        num_scalar_prefetch=1,
        grid=(B, n_pages // P),
        in_specs=[pl.BlockSpec((1, H * T, C), lambda b, j, pt: (b, 0, 0)),
                  pl.BlockSpec((1, T, C), lambda b, j, pt: (b, 0, 0))]
        + _page_specs((None, None, C, PAGE_SIZE), n_pages, (layer,)),
        out_specs=pl.BlockSpec((1, H * T, MLA_KV_LORA), lambda b, j, pt: (b, 0, 0)),
        scratch_shapes=[pltpu.VMEM((H * T, 1), jnp.float32), pltpu.VMEM((H * T, 1), jnp.float32),
                        pltpu.VMEM((H * T, MLA_KV_LORA), jnp.float32)],
    )
    o = pl.pallas_call(
        partial(_mla_decode_body, n_tok=T),
        grid_spec=grid_spec,
        out_shape=jax.ShapeDtypeStruct((B, H * T, MLA_KV_LORA), jnp.float32),
        compiler_params=pltpu.CompilerParams(
            dimension_semantics=("arbitrary", "arbitrary"), vmem_limit_bytes=VMEM_LIMIT),
        name="mla_decode",
    )(page_table.reshape(-1), q_rows, kv_new, *([cache_t] * P))
    return o.reshape(B, H, T, MLA_KV_LORA).transpose(0, 2, 1, 3)


def _diff_decode_body(pt_ref, q_ref, kn_ref, vn_ref, slope_ref, lam_ref, gsub_ref, *rest, n_tok, pos0, out_scale,
                      layer):
    P = PAGES_PER_STEP
    pages, (o_ref, m_ref, l_ref, acc_ref) = rest[:P], rest[P:]
    j = pl.program_id(1)
    G = q_ref.shape[1]
    k_row = lambda g: layer * 2 * G + g
    v_row = lambda g: layer * 2 * G + G + g
    rows = q_ref.shape[2]
    tq = lax.broadcasted_iota(jnp.int32, (rows, 1), 0) % n_tok

    @pl.when(j == 0)
    def _():
        m_ref[...] = jnp.full(m_ref.shape, NEG_INF, jnp.float32)
        l_ref[...] = jnp.zeros(l_ref.shape, jnp.float32)
        acc_ref[...] = jnp.zeros(acc_ref.shape, jnp.float32)

    for g in range(G):
        q = q_ref[0, g].astype(jnp.bfloat16)
        slope = slope_ref[g]
        kpos = (j * P * PAGE_SIZE
                + lax.broadcasted_iota(jnp.int32, (1, P * PAGE_SIZE), 1))
        dist = (pos0 + tq - kpos).astype(jnp.float32)
        s = jnp.concatenate([_dot_nt(q, pg[:, k_row(g), :].astype(jnp.bfloat16)) for pg in pages], axis=-1)
        s = s * (DIFF_D ** -0.5) - slope * dist
        _softmax_update(s, [pg[:, v_row(g), :] for pg in pages], m_ref.at[g], l_ref.at[g], acc_ref.at[g])

    @pl.when(j == pl.num_programs(1) - 1)
    def _():
        lam = lam_ref[...]
        for g in range(G):
            q = q_ref[0, g].astype(jnp.bfloat16)
            sn = _dot_nt(q, kn_ref[0, g].astype(jnp.bfloat16)) * (DIFF_D ** -0.5)
            dist = tq - lax.broadcasted_iota(jnp.int32, sn.shape, 1)
            sn = jnp.where(dist >= 0, sn - slope_ref[g] * dist.astype(jnp.float32), NEG_INF)
            _softmax_update(sn, [vn_ref[0, g]], m_ref.at[g], l_ref.at[g], acc_ref.at[g])
            l = l_ref[g]
            pn = acc_ref[g] / jnp.where(l > 0, l, 1.0)
            R = rows // (2 * n_tok)
            pn = pn.reshape(R, 2, n_tok, pn.shape[-1])
            o = (pn[:, 0] - lam * pn[:, 1]).reshape(R * n_tok, pn.shape[-1])
            o = o * lax.rsqrt(jnp.mean(o * o, axis=-1, keepdims=True) + EPS)
            o_ref[0, g] = o * gsub_ref[...] * out_scale


def _diff_decode(q, k, v, cache_diff, page_table, layer, lam, g_sub, lam_init):
    B, T, _ = q.shape
    G, R, d2 = DIFF_KV_HEADS, DIFF_HEADS // DIFF_KV_HEADS, 2 * DIFF_D
    n_pages = page_table.shape[1]
    P = PAGES_PER_STEP
    assert n_pages % P == 0
    rows = R * 2 * T
    q6 = q.reshape(B, T, G, R, 2, DIFF_D).transpose(0, 2, 3, 4, 1, 5)
    zero = jnp.zeros_like(q6[:, :, :, 0])
    q_pad = jnp.stack([jnp.concatenate([q6[:, :, :, 0], zero], axis=-1),
                       jnp.concatenate([zero, q6[:, :, :, 1]], axis=-1)], axis=3)
    q_pad = q_pad.reshape(B, G, rows, d2)
    by_group = lambda t: t.reshape(B, T, G, d2).transpose(0, 2, 1, 3)
    slopes = jnp.broadcast_to(_alibi_slopes(DIFF_HEADS).reshape(G, R, 1, 1), (G, R, 2 * T, 1)).reshape(G, rows, 1)
    n_layers = cache_diff.shape[2]
    cache_t = cache_diff.reshape(cache_diff.shape[0], PAGE_SIZE, n_layers * 2 * G, d2)
    grid_spec = pltpu.PrefetchScalarGridSpec(
        num_scalar_prefetch=1,
        grid=(B, n_pages // P),
        in_specs=[pl.BlockSpec((1, G, rows, d2), lambda b, j, pt: (b, 0, 0, 0)),
                  pl.BlockSpec((1, G, T, d2), lambda b, j, pt: (b, 0, 0, 0)),
                  pl.BlockSpec((1, G, T, d2), lambda b, j, pt: (b, 0, 0, 0)),
                  pl.BlockSpec((G, rows, 1), lambda b, j, pt: (0, 0, 0)),
                  pl.BlockSpec((1, 1), lambda b, j, pt: (0, 0)),
                  pl.BlockSpec((1, d2), lambda b, j, pt: (0, 0))]
        + _page_specs((None, PAGE_SIZE, n_layers * 2 * G, d2), n_pages, ()),
        out_specs=pl.BlockSpec((1, G, R * T, d2), lambda b, j, pt: (b, 0, 0, 0)),
        scratch_shapes=[pltpu.VMEM((G, rows, 1), jnp.float32), pltpu.VMEM((G, rows, 1), jnp.float32),
                        pltpu.VMEM((G, rows, d2), jnp.float32)],
    )
    o = pl.pallas_call(
        partial(_diff_decode_body, n_tok=T, pos0=n_pages * PAGE_SIZE, out_scale=1.0 - lam_init, layer=layer),
        grid_spec=grid_spec,
        out_shape=jax.ShapeDtypeStruct((B, G, R * T, d2), jnp.float32),
        compiler_params=pltpu.CompilerParams(
            dimension_semantics=("arbitrary", "arbitrary"), vmem_limit_bytes=VMEM_LIMIT),
        name="diff_decode",
    )(page_table.reshape(-1), q_pad, by_group(k), by_group(v), slopes, lam.reshape(1, 1), g_sub.reshape(1, d2),
      *([cache_t] * P))
    return o.reshape(B, G, R, T, d2).transpose(0, 3, 1, 2, 4).reshape(B, T, G * R * d2)


CMP_PAGES_PER_STEP = 32


def _cmp_proj_body(x_ref, perm_ref, pe_ref, w_ref, o_ref, pa_ref):
    PB, _, G, dk, page = x_ref.shape
    n_str = page // CMP_STRIDE
    perm = perm_ref[...]

    def regroup(p, carry):
        for kind in range(2):
            for g in range(G):
                xt = x_ref[p, kind, g]
                hi = xt.astype(jnp.bfloat16)
                r1 = xt - hi.astype(jnp.float32)
                mid = r1.astype(jnp.bfloat16)
                lo = (r1 - mid.astype(jnp.float32)).astype(jnp.bfloat16)
                pa_ref[kind, g, p] = _dot_nt(perm, hi) + _dot_nt(perm, mid) + _dot_nt(perm, lo)
        return carry

    lax.fori_loop(0, PB, regroup, 0)
    for kind in range(2):
        for g in range(G):
            acc = [jnp.zeros((PB * n_str, CMP_HIDDEN), jnp.float32) for _ in range(2)]
            for k in range(CMP_STRIDE // 2):
                a = pa_ref[kind, g, :, (2 * k) * n_str:(2 * k + 1) * n_str, :].reshape(PB * n_str, dk)
                b = pa_ref[kind, g, :, (2 * k + 1) * n_str:(2 * k + 2) * n_str, :].reshape(PB * n_str, dk)
                v = jnp.concatenate([a, b], axis=-1)
                for half in range(2):
                    acc[half] = acc[half] + jnp.dot((v + pe_ref[kind, half, k]).astype(jnp.bfloat16),
                                                    w_ref[kind, half, k], preferred_element_type=jnp.float32)
            o_ref[kind, g] = jnp.concatenate(acc, axis=-1)


def _cmp_proj(cache_nsa, layer, pe, w1):
    POOL = cache_nsa.shape[0]
    G, dk, J = NSA_GROUPS, NSA_DK, CMP_STRIDE
    n_str = PAGE_SIZE // J
    PB = CMP_PAGES_PER_STEP
    assert POOL % PB == 0 and CMP_BLOCK == 2 * J and J % 2 == 0
    xt = cache_nsa.transpose(0, 2, 3, 4, 5, 1)
    row = np.arange(PAGE_SIZE)
    perm = (row[None, :] == (row[:, None] % n_str) * J + row[:, None] // n_str)
    pe_f = pe.reshape(2, 2, J // 2, 1, 2 * dk)
    w_f = w1.reshape(2, 2, J // 2, 2 * dk, CMP_HIDDEN).astype(jnp.bfloat16)
    const = lambda shape: pl.BlockSpec(shape, lambda i: (0,) * len(shape))
    out = pl.pallas_call(
        _cmp_proj_body,
        grid=(POOL // PB,),
        in_specs=[pl.BlockSpec((PB, None, 2, G, dk, PAGE_SIZE), lambda i: (i, layer, 0, 0, 0, 0)),
                  const((PAGE_SIZE, PAGE_SIZE)), const(pe_f.shape), const(w_f.shape)],
        out_specs=pl.BlockSpec((2, G, PB * n_str, 2 * CMP_HIDDEN), lambda i: (0, 0, i, 0)),
        out_shape=jax.ShapeDtypeStruct((2, G, POOL * n_str, 2 * CMP_HIDDEN), jnp.float32),
        scratch_shapes=[pltpu.VMEM((2, G, PB, PAGE_SIZE, dk), jnp.float32)],
        compiler_params=pltpu.CompilerParams(dimension_semantics=("arbitrary",), vmem_limit_bytes=VMEM_LIMIT),
        name="nsa_cmp_proj",
    )(xt, jnp.asarray(perm, jnp.bfloat16), pe_f, w_f)
    return out.reshape(2, G, POOL, n_str, 2 * CMP_HIDDEN)


def _nsa_decode_body(pt_ref, q_ref, lohi_ref, b1_ref, w2_ref, rn_ref, wp_ref, wn_ref, gl_ref, slope_ref, ov_ref,
                     et_ref, *rest, n_tok, pos0, n_sel, n_top):
    P = PAGES_PER_STEP
    pages = rest[:P]
    o_ref, sel_ref, oc_ref, m_ref, l_ref, acc_ref, mw_ref, lw_ref, accw_ref = rest[P:]
    j = pl.program_id(1)
    G, rows, dk = q_ref.shape[1:]
    T = n_tok
    R = rows // T
    scale = NSA_DK ** -0.5
    tq = lax.broadcasted_iota(jnp.int32, (T, 1), 0)
    qpos = pos0 + tq

    @pl.when(j == 0)
    def _():
        for ref in (m_ref, mw_ref):
            ref[...] = jnp.full(ref.shape, NEG_INF, jnp.float32)
        for ref in (l_ref, acc_ref, lw_ref, accw_ref):
            ref[...] = jnp.zeros(ref.shape, jnp.float32)
        n_cmp = lohi_ref.shape[3]
        cmp_end = lax.broadcasted_iota(jnp.int32, (1, n_cmp), 1) * CMP_STRIDE + (CMP_BLOCK - 1)
        dist_c = qpos - cmp_end
        for g in range(G):
            kv_cmp = []
            for kind in range(2):
                lohi = lohi_ref[kind, g, 0]
                lo, hi = lohi[:, :CMP_HIDDEN], lohi[:, CMP_HIDDEN:]
                hi_next = pltpu.roll(hi, n_cmp - 1, 0)
                pre = lo + hi_next + b1_ref[kind]
                h = pre * jax.nn.sigmoid(pre)
                kv_cmp.append(jnp.dot(h.astype(jnp.bfloat16), w2_ref[kind], preferred_element_type=jnp.float32))
            k_cmp, v_cmp = kv_cmp
            q = (q_ref[0, g] * scale).astype(jnp.bfloat16)
            s = _dot_nt(q, k_cmp.astype(jnp.bfloat16))
            s3 = s.reshape(R, T, n_cmp) - slope_ref[g] * dist_c.astype(jnp.float32)[None]
            s3 = jnp.where((dist_c >= 0)[None], s3, NEG_INF)
            m_c = jnp.max(s3, axis=-1, keepdims=True)
            e_c = jnp.exp(s3 - jnp.where(m_c == NEG_INF, 0.0, m_c))
            d_c = jnp.sum(e_c, axis=-1, keepdims=True)
            p_c = e_c / jnp.where(d_c > 0, d_c, 1.0)
            oc_ref[g] = jnp.dot(p_c.reshape(rows, n_cmp).astype(jnp.bfloat16), v_cmp.astype(jnp.bfloat16),
                                preferred_element_type=jnp.float32)
            imp = _dot_split3(jnp.sum(p_c, axis=0), ov_ref[...])
            sel_ref[g] = _select_blocks(imp, qpos, n_top, n_sel)

    n_keys = P * PAGE_SIZE
    kpos = j * n_keys + lax.broadcasted_iota(jnp.int32, (1, n_keys), 1)
    dist = (qpos - kpos).astype(jnp.float32)
    for g in range(G):
        q = (q_ref[0, g] * scale).astype(jnp.bfloat16)
        picked = _dot_nt(sel_ref[g].astype(jnp.bfloat16), et_ref[...]) > 0.5
        s = jnp.concatenate([jnp.dot(q, pg[0, g].astype(jnp.bfloat16), preferred_element_type=jnp.float32)
                             for pg in pages], axis=-1)
        s3 = s.reshape(R, T, n_keys) - slope_ref[g] * dist[None]
        s3 = jnp.where(picked[None], s3, NEG_INF)
        _softmax_update(s3.reshape(rows, n_keys), [pg[1, g] for pg in pages], m_ref.at[g], l_ref.at[g], acc_ref.at[g],
                        vt=True)

    @pl.when(j == pl.num_programs(1) - 1)
    def _():
        tk = lax.broadcasted_iota(jnp.int32, (1, T), 1)
        dist_n = tq - tk
        n_win = wp_ref.shape[4]
        dist_p = n_win + tq - lax.broadcasted_iota(jnp.int32, (1, n_win), 1)
        gates = jax.nn.sigmoid(gl_ref[0])
        for g in range(G):
            q = (q_ref[0, g] * scale).astype(jnp.bfloat16)
            slope3 = slope_ref[g]

            def scores(k, d, mask, kt=False):
                k = k.astype(jnp.bfloat16)
                s = jnp.dot(q, k, preferred_element_type=jnp.float32) if kt else _dot_nt(q, k)
                s3 = s.reshape(R, T, s.shape[-1]) - slope3 * d.astype(jnp.float32)[None]
                return jnp.where(mask[None], s3, NEG_INF).reshape(rows, s.shape[-1])

            cur = pos0 // SEL_BLOCK
            cur_picked = sel_ref[g][:, cur:cur + 1] > 0.5
            _softmax_update(scores(rn_ref[0, 2, g], dist_n, (dist_n >= 0) & cur_picked), [rn_ref[0, 3, g]],
                            m_ref.at[g], l_ref.at[g], acc_ref.at[g])
            _softmax_update(scores(wp_ref[0, 0, g], dist_p, (dist_p >= 0) & (dist_p <= WINDOW), kt=True),
                            [wp_ref[0, 1, g]], mw_ref.at[g], lw_ref.at[g], accw_ref.at[g], vt=True)
            _softmax_update(scores(wn_ref[0, 0, g], dist_n, dist_n >= 0), [wn_ref[0, 1, g]],
                            mw_ref.at[g], lw_ref.at[g], accw_ref.at[g])
            l_s, l_w = l_ref[g], lw_ref[g]
            o_s = acc_ref[g] / jnp.where(l_s > 0, l_s, 1.0)
            o_w = accw_ref[g] / jnp.where(l_w > 0, l_w, 1.0)
            gt = gates[g]
            o_ref[0, g] = gt[:, 0:1] * oc_ref[g] + gt[:, 1:2] * o_s + gt[:, 2:3] * o_w


def _nsa_decode(q, rows_new, win_new, gl, cache_nsa, state_win, page_table, layer, pe, w1, b1, w2):
    B, T, _ = q.shape
    G, R, dk = NSA_GROUPS, NSA_HEADS // NSA_GROUPS, NSA_DK
    n_pages = page_table.shape[1]
    P = PAGES_PER_STEP
    pos0 = n_pages * PAGE_SIZE
    S = pos0 + T
    assert n_pages % P == 0 and T <= CMP_STRIDE and T <= SEL_BLOCK and pos0 % SEL_BLOCK == 0
    assert state_win.shape[2] == WINDOW
    n_str = PAGE_SIZE // CMP_STRIDE
    n_cmp = n_pages * n_str
    n_sel = -(-S // SEL_BLOCK)
    n_sel_pad = _round_up(n_sel, LANE)
    n_top = min(SEL_TOP, n_sel)
    rows = R * T
    lohi = _cmp_proj(cache_nsa, layer, pe, w1)
    lohi = lohi[:, :, page_table].reshape(2, G, B, n_cmp, 2 * CMP_HIDDEN)
    q_rows = q.reshape(B, T, G, R, dk).transpose(0, 2, 3, 1, 4).reshape(B, G, rows, dk)
    gl_rows = gl.reshape(B, T, G, R, 3).transpose(0, 2, 3, 1, 4).reshape(B, G, rows, 3)
    slopes = _alibi_slopes(NSA_HEADS).reshape(G, R, 1, 1)
    cmp_start = np.arange(n_cmp) * CMP_STRIDE
    sel_start = np.arange(n_sel_pad) * SEL_BLOCK
    overlap = ((cmp_start[:, None] < sel_start[None, :] + SEL_BLOCK)
               & (cmp_start[:, None] + CMP_BLOCK - 1 >= sel_start[None, :]) & (np.arange(n_cmp)[:, None] < n_cmp - 1)
               & (np.arange(n_sel_pad)[None, :] < n_sel))
    expand_t = (jnp.arange(pos0)[:, None] // SEL_BLOCK) == jnp.arange(n_sel_pad)[None, :]
    cache_t = cache_nsa.transpose(0, 2, 3, 4, 5, 1)
    win_t = state_win.transpose(0, 1, 3, 4, 5, 2)
    sel_pages = _page_specs((None, None, 2, G, dk, PAGE_SIZE), n_pages, (layer, 1))
    per_b = lambda shape: pl.BlockSpec((1,) + shape, lambda b, j, pt: (b,) + (0,) * len(shape))
    const = lambda shape: pl.BlockSpec(shape, lambda b, j, pt: (0,) * len(shape))
    grid_spec = pltpu.PrefetchScalarGridSpec(
        num_scalar_prefetch=1,
        grid=(B, n_pages // P),
        in_specs=[per_b((G, rows, dk)),
                  pl.BlockSpec((2, G, 1, n_cmp, 2 * CMP_HIDDEN), lambda b, j, pt: (0, 0, b, 0, 0)),
                  const((2, 1, CMP_HIDDEN)), const((2, CMP_HIDDEN, dk)),
                  per_b((4, G, T, dk)),
                  pl.BlockSpec((1, None, 2, G, dk, WINDOW), lambda b, j, pt: (b, layer, 0, 0, 0, 0)),
                  per_b((2, G, T, dk)), per_b((G, rows, 3)),
                  const((G, R, 1, 1)), const((n_cmp, n_sel_pad)),
                  pl.BlockSpec((P * PAGE_SIZE, n_sel_pad), lambda b, j, pt: (j, 0))]
        + sel_pages,
        out_specs=per_b((G, rows, dk)),
        scratch_shapes=[pltpu.VMEM((G, T, n_sel_pad), jnp.float32), pltpu.VMEM((G, rows, dk), jnp.float32),
                        pltpu.VMEM((G, rows, 1), jnp.float32), pltpu.VMEM((G, rows, 1), jnp.float32),
                        pltpu.VMEM((G, rows, dk), jnp.float32),
                        pltpu.VMEM((G, rows, 1), jnp.float32), pltpu.VMEM((G, rows, 1), jnp.float32),
                        pltpu.VMEM((G, rows, dk), jnp.float32)],
    )
    o = pl.pallas_call(
        partial(_nsa_decode_body, n_tok=T, pos0=pos0, n_sel=n_sel, n_top=n_top),
        grid_spec=grid_spec,
        out_shape=jax.ShapeDtypeStruct((B, G, rows, dk), jnp.float32),
        compiler_params=pltpu.CompilerParams(
            dimension_semantics=("arbitrary", "arbitrary"), vmem_limit_bytes=VMEM_LIMIT),
        name="nsa_decode",
    )(page_table.reshape(-1), q_rows, lohi, b1.reshape(2, 1, CMP_HIDDEN), w2.astype(jnp.bfloat16),
      rows_new.transpose(0, 2, 3, 1, 4), win_t, win_new.transpose(0, 2, 3, 1, 4),
      gl_rows, slopes, jnp.asarray(overlap, jnp.bfloat16), expand_t.astype(jnp.bfloat16),
      *([cache_t] * P))
    return o.reshape(B, G, R, T, dk).transpose(0, 3, 1, 2, 4).reshape(B, T, G * R * dk)


def _rms(x, g):
    xf = x.astype(jnp.float32)
    y = xf * lax.rsqrt(jnp.mean(xf * xf, axis=-1, keepdims=True) + EPS)
    return (y * g.astype(jnp.float32)).astype(x.dtype)


def _split(x, sizes):
    return jnp.split(x, np.cumsum(sizes)[:-1].tolist(), axis=-1)


def _alibi_slopes(n):
    return 2.0 ** (-8.0 * jnp.arange(1, n + 1, dtype=jnp.float32) / n)


def _rope(x, pos):
    half = x.shape[-1] // 2
    freq = ROPE_THETA ** (-jnp.arange(half, dtype=jnp.float32) / half)
    ang = pos.astype(jnp.float32)[:, None] * freq[None, :]
    shape = (1, ang.shape[0]) + (1,) * (x.ndim - 3) + (half,)
    cos, sin = jnp.cos(ang).reshape(shape), jnp.sin(ang).reshape(shape)
    xf = x.astype(jnp.float32)
    x1, x2 = xf[..., :half], xf[..., half:]
    return jnp.concatenate([x1 * cos - x2 * sin, x2 * cos + x1 * sin], axis=-1).astype(x.dtype)


def _masked_softmax(s, mask):
    s = jnp.where(mask, s, -jnp.inf)
    m = jnp.max(s, axis=-1, keepdims=True)
    m = jnp.where(jnp.isfinite(m), m, 0.0)
    e = jnp.exp(s - m)
    d = jnp.sum(e, axis=-1, keepdims=True)
    return e / jnp.where(d > 0, d, 1.0)


def _sweep(fn, n_q, block=Q_BLOCK):
    qb = block if n_q % block == 0 else n_q
    nb = n_q // qb
    if nb == 1:
        return fn(0, qb)
    out = lax.map(lambda i: fn(i * qb, qb), jnp.arange(nb))
    out = jnp.moveaxis(out, 0, 1)
    return out.reshape((out.shape[0], nb * qb) + out.shape[3:])


def _gather_pages(cache, page_table, layer):
    g = cache[page_table, :, layer]
    return g.reshape((g.shape[0], g.shape[1] * g.shape[2]) + g.shape[3:])


def _mla(cq, ckv, kr, pos, kv_past, g_q, w_uq, g_kv, w_uk, w_uv, paged=None):
    q = jnp.einsum('btc,chd->bthd', _rms(cq, g_q), w_uq)
    q_nope, q_rope = q[..., :MLA_NOPE], _rope(q[..., MLA_NOPE:], pos)
    kv_new = jnp.concatenate([_rms(ckv, g_kv), _rope(kr, pos)], axis=-1)
    q_cat = jnp.concatenate([jnp.einsum('bthd,rhd->bthr', q_nope, w_uk), q_rope], axis=-1)
    if paged is not None or kv_past is None:
        o_lat = _mla_decode(q_cat, kv_new, *paged) if paged is not None else _mla_prompt(q_cat, kv_new)
        o = jnp.einsum('bthr,rhv->bthv', o_lat, w_uv)
        return o.reshape(o.shape[:2] + (-1,)), kv_new
    kv_all = kv_new if kv_past is None else jnp.concatenate([kv_past, kv_new], axis=1)
    lat = kv_all[..., :MLA_KV_LORA]
    k_pos = jnp.arange(kv_all.shape[1])
    scale = (MLA_NOPE + MLA_ROPE) ** -0.5

    def block(s0, qb):
        qc = lax.dynamic_slice_in_dim(q_cat, s0, qb, axis=1)
        qp = lax.dynamic_slice_in_dim(pos, s0, qb)
        s = jnp.einsum('bqhc,bkc->bhqk', qc, kv_all).astype(jnp.float32) * scale
        p = _masked_softmax(s, k_pos[None, :] <= qp[:, None])
        return jnp.einsum('bhqk,bkr->bqhr', p.astype(lat.dtype), lat)

    o_lat = _sweep(block, q_cat.shape[1])
    o = jnp.einsum('bthr,rhv->bthv', o_lat, w_uv)
    return o.reshape(o.shape[:2] + (-1,)), kv_new


def _diff(q, k, v, pos, kv_past, lam_p, g_sub, lam_init, paged=None):
    B, T = q.shape[:2]
    G, R = DIFF_KV_HEADS, DIFF_HEADS // DIFF_KV_HEADS
    kv_new = jnp.stack([k.reshape(B, T, G, 2 * DIFF_D), v.reshape(B, T, G, 2 * DIFF_D)], axis=2)
    lp = lam_p.astype(jnp.float32)
    lam = jnp.exp(jnp.sum(lp[0] * lp[1])) - jnp.exp(jnp.sum(lp[2] * lp[3])) + lam_init
    if paged is not None:
        return _diff_decode(q, k, v, *paged, lam, g_sub, lam_init), kv_new
    if kv_past is None:
        return _diff_prompt(q, k, v, lam, g_sub, lam_init), kv_new
    q = q.reshape(B, T, G, R, 2, DIFF_D)
    kv_all = kv_new if kv_past is None else jnp.concatenate([kv_past, kv_new], axis=1)
    Tk = kv_all.shape[1]
    k_all = kv_all[:, :, 0].reshape(B, Tk, G, 2, DIFF_D)
    v_all = kv_all[:, :, 1]
    k_pos = jnp.arange(Tk)
    slopes = _alibi_slopes(DIFF_HEADS).reshape(1, G, R, 1, 1, 1)

    def block(s0, qb):
        qc = lax.dynamic_slice_in_dim(q, s0, qb, axis=1)
        qp = lax.dynamic_slice_in_dim(pos, s0, qb)
        dist = qp[:, None] - k_pos[None, :]
        s = jnp.einsum('bqgrcd,bkgcd->bgrcqk', qc, k_all).astype(jnp.float32) * (DIFF_D ** -0.5)
        p = _masked_softmax(s - slopes * dist.astype(jnp.float32), dist >= 0)
        a = p[:, :, :, 0] - lam * p[:, :, :, 1]
        return jnp.einsum('bgrqk,bkge->bqgre', a.astype(v_all.dtype), v_all)

    o = _sweep(block, T)
    o = _rms(o, g_sub) * (1.0 - lam_init)
    return o.reshape(B, T, -1), kv_new


def _compress(x, pe, w1, b1, w2):
    B, S, G, dk = x.shape
    nc = S // CMP_STRIDE
    xc = x[:, :nc * CMP_STRIDE].reshape(B, nc, CMP_STRIDE, G, dk)
    lo = jnp.einsum('bnjgd,jdh->bngh', xc + pe[:CMP_STRIDE, None, :], w1[:CMP_STRIDE])
    hi = jnp.einsum('bnjgd,jdh->bngh', xc + pe[CMP_STRIDE:, None, :], w1[CMP_STRIDE:])
    h = jax.nn.silu(lo[:, :-1] + hi[:, 1:] + b1)
    return jnp.einsum('bngh,hd->bngd', h, w2)


def _nsa(q, rows_new, win_new, gate_logits, pos, pos0, rows_past, win_past, pe, w1, b1, w2):
    B, T = q.shape[:2]
    G, R = NSA_GROUPS, NSA_HEADS // NSA_GROUPS
    q = q.reshape(B, T, G, R, NSA_DK)
    gates = jax.nn.sigmoid(gate_logits.astype(jnp.float32)).reshape(B, T, G, R, 3)
    rows = rows_new if rows_past is None else jnp.concatenate([rows_past, rows_new], axis=1)
    S = rows.shape[1]
    if win_past is None:
        win = jnp.pad(win_new, ((0, 0), (WINDOW, 0), (0, 0), (0, 0), (0, 0)))
        win_pos0 = -WINDOW
        win_state = win_new[:, T - min(WINDOW, T):]
    else:
        win = jnp.concatenate([win_past, win_new], axis=1)
        win_pos0 = pos0 - win_past.shape[1]
        win_state = win[:, T:]
    span = win.shape[1] - T
    k_cmp = _compress(rows[:, :, 0], pe[0], w1[0], b1[0], w2[0])
    v_cmp = _compress(rows[:, :, 1], pe[1], w1[1], b1[1], w2[1])
    n_cmp = k_cmp.shape[1]
    cmp_start = jnp.arange(n_cmp) * CMP_STRIDE
    cmp_end = cmp_start + CMP_BLOCK - 1
    n_sel = -(-S // SEL_BLOCK)
    sel_start = jnp.arange(n_sel) * SEL_BLOCK
    overlap = ((cmp_start[:, None] < sel_start[None, :] + SEL_BLOCK)
               & (cmp_end[:, None] >= sel_start[None, :])).astype(jnp.float32)
    sel = jnp.pad(rows[:, :, 2:], ((0, 0), (0, n_sel * SEL_BLOCK - S), (0, 0), (0, 0), (0, 0)))
    sel = sel.reshape(B, n_sel, SEL_BLOCK, 2, G, NSA_DK).transpose(0, 4, 3, 1, 2, 5)
    n_top = min(SEL_TOP, n_sel)
    slopes = _alibi_slopes(NSA_HEADS).reshape(G, R)
    scale = NSA_DK ** -0.5
    take_blocks = jax.vmap(jax.vmap(lambda blk, ix: blk[:, ix]))
    j_sel = jnp.arange(n_sel)

    def block(s0, qb):
        qc = lax.dynamic_slice_in_dim(q, s0, qb, axis=1)
        qp = lax.dynamic_slice_in_dim(pos, s0, qb)
        gc = lax.dynamic_slice_in_dim(gates, s0, qb, axis=1)
        dist_c = qp[:, None] - cmp_end[None, :]
        s = jnp.einsum('bqgrd,bngd->bgrqn', qc, k_cmp).astype(jnp.float32) * scale
        p_c = _masked_softmax(s - slopes[None, :, :, None, None] * dist_c.astype(jnp.float32), dist_c >= 0)
        o_c = jnp.einsum('bgrqn,bngd->bqgrd', p_c.astype(v_cmp.dtype), v_cmp)
        imp = jnp.einsum('bgrqn,ns->bgqs', p_c, overlap)
        cur = qp // SEL_BLOCK
        valid = sel_start[None, :] <= qp[:, None]
        forced = (j_sel[None, :] == 0) | (j_sel[None, :] == cur[:, None]) | (j_sel[None, :] == cur[:, None] - 1)
        score = jnp.where(valid, jnp.where(forced, jnp.inf, imp), -jnp.inf)
        _, idx = lax.top_k(score, n_top)
        kv_sel = take_blocks(sel, idx)
        kpos = idx[..., None] * SEL_BLOCK + jnp.arange(SEL_BLOCK)
        dist_s = qp[None, None, :, None, None] - kpos
        s = jnp.einsum('bqgrd,bgqnld->bgrqnl', qc, kv_sel[:, :, 0]).astype(jnp.float32) * scale
        s = s - slopes[None, :, :, None, None, None] * dist_s[:, :, None].astype(jnp.float32)
        sh = s.shape
        p_s = _masked_softmax(s.reshape(sh[:4] + (-1,)),
                              (dist_s >= 0)[:, :, None].reshape(sh[0], sh[1], 1, sh[3], -1)).reshape(sh)
        o_s = jnp.einsum('bgrqnl,bgqnld->bqgrd', p_s.astype(kv_sel.dtype), kv_sel[:, :, 1])
        w = lax.dynamic_slice_in_dim(win, s0, span + qb, axis=1)
        wpos = win_pos0 + s0 + jnp.arange(span + qb)
        dist_w = qp[:, None] - wpos[None, :]
        mask_w = (wpos[None, :] >= 0) & (dist_w >= 0) & (dist_w <= WINDOW)
        s = jnp.einsum('bqgrd,bkgd->bgrqk', qc, w[:, :, 0]).astype(jnp.float32) * scale
        p_w = _masked_softmax(s - slopes[None, :, :, None, None] * dist_w.astype(jnp.float32), mask_w)
        o_w = jnp.einsum('bgrqk,bkgd->bqgrd', p_w.astype(w.dtype), w[:, :, 1])
        return (gc[..., 0:1] * o_c + gc[..., 1:2] * o_s + gc[..., 2:3] * o_w).astype(q.dtype)

    o = _sweep(block, T, SEL_Q_BLOCK)
    return o.reshape(B, T, -1), win_state


def _pad_cols(w, n):
    return jnp.pad(w, ((0, 0), (0, n - w.shape[1])))


def _run(x, c, p, cache_mla=None, cache_diff=None, cache_nsa=None, state_win=None, page_table=None):
    B, T, _ = x.shape
    has_past = page_table is not None
    pos0 = page_table.shape[1] * PAGE_SIZE if has_past else 0
    pos = pos0 + jnp.arange(T)
    mod = jnp.einsum('bc,lcd->lbd', jax.nn.silu(c), p['ada_w']) + p['ada_b'][:, None, :]
    mla_rows, diff_rows, nsa_rows, win_rows = [], [], [], []
    h = x
    n_ab, n_c = sum(AB_SPLIT), sum(C_SPLIT)
    for l in range(DEPTH):
        shift, scale, gate = jnp.split(mod[l][:, None, :], 3, axis=-1)
        pre_g = p['pre_g'][l].reshape(1, 1, D_MODEL)
        post_g = p['post_g'][l].reshape(1, 1, D_MODEL)
        i = l // 2
        if l % 2 == 0:
            w_in = _pad_cols(p['ab_w_in'][i], _round_up(n_ab, LANE)).astype(jnp.bfloat16)
            proj = _inproj(h, scale, shift, pre_g, w_in)[..., :n_ab]
            cq, ckv, kr, ga, qd, kd, vd, gb = _split(proj, AB_SPLIT)
            o_a, kv_a = _mla(cq, ckv, kr, pos, None,
                             p['mla_q_g'][i], p['mla_w_uq'][i], p['mla_kv_g'][i], p['mla_w_uk'][i], p['mla_w_uv'][i],
                             paged=(cache_mla, page_table, i) if has_past else None)
            o_b, kv_b = _diff(qd, kd, vd, pos, None,
                              p['diff_lam'][i], p['diff_sub_g'][i], 0.8 - 0.6 * math.exp(-0.3 * l),
                              paged=(cache_diff, page_table, i) if has_past else None)
            o = jnp.concatenate([o_a, o_b], axis=-1)
            gsrc = jnp.concatenate([ga, gb], axis=-1)
            h = _outproj(o, gsrc, h, gate, post_g, p['ab_w_out'][i].astype(jnp.bfloat16))
            mla_rows.append(kv_a)
            diff_rows.append(kv_b)
        else:
            w_in = _pad_cols(p['c_w_in'][i], _round_up(n_c, LANE)).astype(jnp.bfloat16)
            proj = _inproj(h, scale, shift, pre_g, w_in)[..., :n_c]
            qn, kc, vc, ks, vs, kw, vw, gl, gn = _split(proj, C_SPLIT)
            grp = lambda t: t.reshape(B, T, NSA_GROUPS, NSA_DK)
            rows_new = jnp.stack([grp(kc), grp(vc), grp(ks), grp(vs)], axis=2)
            win_new = jnp.stack([grp(kw), grp(vw)], axis=2)
            if has_past:
                o_n = _nsa_decode(qn, rows_new, win_new, gl, cache_nsa, state_win, page_table, i,
                                  p['nsa_pe'][i], p['nsa_w1'][i], p['nsa_b1'][i], p['nsa_w2'][i])
                win_state = jnp.concatenate([state_win[:, i], win_new], axis=1)[:, T:]
            else:
                pe, w1, b1, w2 = p['nsa_pe'][i], p['nsa_w1'][i], p['nsa_b1'][i], p['nsa_w2'][i]
                k_cmp = _compress(grp(kc), pe[0], w1[0], b1[0], w2[0])
                v_cmp = _compress(grp(vc), pe[1], w1[1], b1[1], w2[1])
                o_n = _nsa_prompt(qn, k_cmp, v_cmp, ks, vs, kw, vw, gl)
                win_state = win_new[:, T - min(WINDOW, T):]
            h = _outproj(o_n, gn, h, gate, post_g, p['c_w_out'][i].astype(jnp.bfloat16))
            nsa_rows.append(rows_new)
            win_rows.append(win_state)
    return (h, jnp.stack(mla_rows, axis=2), jnp.stack(diff_rows, axis=2),
            jnp.stack(nsa_rows, axis=2), jnp.stack(win_rows, axis=1))


def kernel(x_prompt, x_sample, c_prompt, c_sample, cache_mla, cache_diff, cache_nsa, state_nsa_win, page_table,
           ada_w, ada_b, pre_g, post_g, ab_w_in, mla_q_g, mla_w_uq, mla_kv_g, mla_w_uk, mla_w_uv,
           diff_lam, diff_sub_g, ab_w_out, c_w_in, nsa_pe, nsa_w1, nsa_b1, nsa_w2, c_w_out):
    p = dict(ada_w=ada_w, ada_b=ada_b, pre_g=pre_g, post_g=post_g, ab_w_in=ab_w_in,
             mla_q_g=mla_q_g, mla_w_uq=mla_w_uq, mla_kv_g=mla_kv_g, mla_w_uk=mla_w_uk, mla_w_uv=mla_w_uv,
             diff_lam=diff_lam, diff_sub_g=diff_sub_g, ab_w_out=ab_w_out, c_w_in=c_w_in,
             nsa_pe=nsa_pe, nsa_w1=nsa_w1, nsa_b1=nsa_b1, nsa_w2=nsa_w2, c_w_out=c_w_out)
    y_prompt, mla_p, diff_p, nsa_p, win_p = _run(x_prompt, c_prompt, p)
    y_sample, mla_s, diff_s, nsa_s, win_s = _run(x_sample, c_sample, p, cache_mla, cache_diff, cache_nsa,
                                                 state_nsa_win, page_table)
    return (y_prompt, y_sample, mla_p, mla_s, diff_p, diff_s, nsa_p, nsa_s, win_p, win_s)
```

```python
import math
from functools import partial

import jax
import jax.numpy as jnp
import numpy as np
from jax import lax
from jax.experimental import pallas as pl
from jax.experimental.pallas import tpu as pltpu

D_MODEL = 1024
DEPTH = 4
PAGE_SIZE = 128
MLA_HEADS = 8
MLA_NOPE = 64
MLA_ROPE = 32
MLA_V = 64
MLA_Q_LORA = 768
MLA_KV_LORA = 256
ROPE_THETA = 10000.0
DIFF_HEADS = 4
DIFF_KV_HEADS = 2
DIFF_D = 64
NSA_HEADS = 16
NSA_GROUPS = 2
NSA_DK = 64
NSA_DV = 64
CMP_BLOCK = 32
CMP_STRIDE = 16
CMP_HIDDEN = 128
SEL_BLOCK = 64
SEL_TOP = 16
WINDOW = 512
Q_BLOCK = 128
SEL_Q_BLOCK = 32
EPS = 1e-6

AB_SPLIT = (MLA_Q_LORA, MLA_KV_LORA, MLA_ROPE, MLA_HEADS * MLA_V,
            DIFF_HEADS * 2 * DIFF_D, DIFF_KV_HEADS * 2 * DIFF_D, DIFF_KV_HEADS * 2 * DIFF_D, DIFF_HEADS * 2 * DIFF_D)
C_SPLIT = (NSA_HEADS * NSA_DK,) + (NSA_GROUPS * NSA_DK,) * 6 + (3 * NSA_HEADS, NSA_HEADS * NSA_DV)

LANE = 128
ROW_TILE = 512
VMEM_LIMIT = 56 * 1024 * 1024


def _round_up(n, m):
    return -(-n // m) * m


def _inproj_body(h_ref, scale_ref, shift_ref, g_ref, w_ref, o_ref):
    bb, tt, d = h_ref.shape
    h = h_ref[...]
    y = h * lax.rsqrt(jnp.mean(h * h, axis=-1, keepdims=True) + EPS)
    u = (y * g_ref[...]) * (1.0 + scale_ref[...]) + shift_ref[...]
    u = u.reshape(bb * tt, d).astype(jnp.bfloat16)
    o = jnp.dot(u, w_ref[...], preferred_element_type=jnp.float32)
    o_ref[...] = o.reshape(bb, tt, o.shape[-1])


def _row_blocks(B, T):
    tt = min(T, ROW_TILE)
    bb = max(1, ROW_TILE // tt)
    bb = min(bb, B)
    assert T % tt == 0 and B % bb == 0
    return bb, tt


def _inproj(h, scale, shift, g, w_bf16):
    B, T, D = h.shape
    N = w_bf16.shape[1]
    bb, tt = _row_blocks(B, T)
    return pl.pallas_call(
        _inproj_body,
        grid=(B // bb, T // tt),
        in_specs=[
            pl.BlockSpec((bb, tt, D), lambda i, j: (i, j, 0)),
            pl.BlockSpec((bb, 1, D), lambda i, j: (i, 0, 0)),
            pl.BlockSpec((bb, 1, D), lambda i, j: (i, 0, 0)),
            pl.BlockSpec((1, 1, D), lambda i, j: (0, 0, 0)),
            pl.BlockSpec((D, N), lambda i, j: (0, 0)),
        ],
        out_specs=pl.BlockSpec((bb, tt, N), lambda i, j: (i, j, 0)),
        out_shape=jax.ShapeDtypeStruct((B, T, N), jnp.float32),
        compiler_params=pltpu.CompilerParams(
            dimension_semantics=("arbitrary", "arbitrary"), vmem_limit_bytes=VMEM_LIMIT),
        name="inproj",
    )(h, scale, shift, g, w_bf16)


def _outproj_body(o_ref, gsrc_ref, h_ref, gate_ref, pg_ref, w_ref, out_ref):
    bb, tt, d = h_ref.shape
    gs = gsrc_ref[...]
    a = o_ref[...] * (gs * jax.nn.sigmoid(gs))
    a = a.reshape(bb * tt, a.shape[-1]).astype(jnp.bfloat16)
    y = jnp.dot(a, w_ref[...], preferred_element_type=jnp.float32)
    y = y * lax.rsqrt(jnp.mean(y * y, axis=-1, keepdims=True) + EPS)
    y = y.reshape(bb, tt, d) * pg_ref[...]
    out_ref[...] = h_ref[...] + gate_ref[...] * y


def _outproj(o, gsrc, h, gate, pg, w_bf16):
    B, T, D = h.shape
    W = o.shape[-1]
    bb, tt = _row_blocks(B, T)
    return pl.pallas_call(
        _outproj_body,
        grid=(B // bb, T // tt),
        in_specs=[
            pl.BlockSpec((bb, tt, W), lambda i, j: (i, j, 0)),
            pl.BlockSpec((bb, tt, W), lambda i, j: (i, j, 0)),
            pl.BlockSpec((bb, tt, D), lambda i, j: (i, j, 0)),
            pl.BlockSpec((bb, 1, D), lambda i, j: (i, 0, 0)),
            pl.BlockSpec((1, 1, D), lambda i, j: (0, 0, 0)),
            pl.BlockSpec((W, D), lambda i, j: (0, 0)),
        ],
        out_specs=pl.BlockSpec((bb, tt, D), lambda i, j: (i, j, 0)),
        out_shape=jax.ShapeDtypeStruct((B, T, D), jnp.float32),
        compiler_params=pltpu.CompilerParams(
            dimension_semantics=("arbitrary", "arbitrary"), vmem_limit_bytes=VMEM_LIMIT),
        name="outproj",
    )(o, gsrc, h, gate, pg, w_bf16)


NSA_TQ = 128
NSA_TK = 256
NEG_INF = float("-inf")


def _dot_nt(a, b):
    return lax.dot_general(a, b, (((1,), (1,)), ((), ())), preferred_element_type=jnp.float32)


def _dot_split3(a, b01):
    hi = a.astype(jnp.bfloat16)
    r1 = a - hi.astype(jnp.float32)
    mid = r1.astype(jnp.bfloat16)
    lo = (r1 - mid.astype(jnp.float32)).astype(jnp.bfloat16)
    dot = partial(jnp.dot, preferred_element_type=jnp.float32)
    return dot(hi, b01) + dot(mid, b01) + dot(lo, b01)


def _flash_step(q, k, v, slope3, dist, mask, carry, scale=None):
    m, l, acc = carry
    R, TQ, _ = m.shape
    s = _dot_nt(q, k.astype(jnp.bfloat16))
    s3 = s.reshape(R, TQ, s.shape[-1])
    if scale is not None:
        s3 = s3 * scale
    if slope3 is not None:
        s3 = s3 - slope3 * dist[None]
    s3 = jnp.where(mask[None], s3, NEG_INF)
    m_new = jnp.maximum(m, jnp.max(s3, axis=-1, keepdims=True))
    m_safe = jnp.where(m_new == NEG_INF, 0.0, m_new)
    p = jnp.exp(s3 - m_safe)
    alpha = jnp.exp(m - m_safe)
    l = alpha * l + jnp.sum(p, axis=-1, keepdims=True)
    pv = jnp.dot(p.reshape(R * TQ, p.shape[-1]).astype(jnp.bfloat16), v.astype(jnp.bfloat16),
                 preferred_element_type=jnp.float32)
    acc = alpha * acc + pv.reshape(R, TQ, pv.shape[-1])
    return m_new, l, acc


def _flash_init(R, TQ, dv):
    return (jnp.full((R, TQ, 1), NEG_INF, jnp.float32), jnp.zeros((R, TQ, 1), jnp.float32),
            jnp.zeros((R, TQ, dv), jnp.float32))


def _flash_out(carry):
    _, l, acc = carry
    return acc / jnp.where(l > 0, l, 1.0)


def _select_blocks(imp, qpos, n_top, n_real=None):
    TQ, n_sel = imp.shape
    n_real = n_sel if n_real is None else n_real
    j = lax.broadcasted_iota(jnp.int32, (TQ, n_sel), 1)
    cur = qpos // SEL_BLOCK
    valid = j * SEL_BLOCK <= qpos
    forced = (j == 0) | (j == cur) | (j == cur - 1)
    score = jnp.where(valid, jnp.where(forced, jnp.inf, imp), NEG_INF)
    rank = jnp.zeros((TQ, n_sel), jnp.int32)
    for k in range(n_real):
        col = score[:, k:k + 1]
        beats = (col > score) | ((col == score) & (j > k))
        rank = rank + beats.astype(jnp.int32)
    return (rank < n_top).astype(jnp.float32)


def _nsa_prompt_body(q_ref, kc_ref, vc_ref, ks_ref, vs_ref, kw_ref, vw_ref, gl_ref, slope_ref, ov_ref, et_ref,
                     o_ref, *, n_top):
    _, _, R, TQ, dk = q_ref.shape
    TK = NSA_TK
    qt = pl.program_id(2)
    s0 = qt * TQ
    q = (q_ref[0, 0].reshape(R * TQ, dk) * (NSA_DK ** -0.5)).astype(jnp.bfloat16)
    slope3 = slope_ref[0]
    qpos = s0 + lax.broadcasted_iota(jnp.int32, (TQ, 1), 0)

    n_cmp = kc_ref.shape[2]
    cmp_end = lax.broadcasted_iota(jnp.int32, (1, n_cmp), 1) * CMP_STRIDE + (CMP_BLOCK - 1)
    dist_c = qpos - cmp_end
    s = _dot_nt(q, kc_ref[0, 0].astype(jnp.bfloat16))
    s3 = s.reshape(R, TQ, n_cmp) - slope3 * dist_c.astype(jnp.float32)[None]
    s3 = jnp.where((dist_c >= 0)[None], s3, NEG_INF)
    m_c = jnp.max(s3, axis=-1, keepdims=True)
    e_c = jnp.exp(s3 - jnp.where(m_c == NEG_INF, 0.0, m_c))
    d_c = jnp.sum(e_c, axis=-1, keepdims=True)
    p_c = e_c / jnp.where(d_c > 0, d_c, 1.0)
    o_c = jnp.dot(p_c.reshape(R * TQ, n_cmp).astype(jnp.bfloat16), vc_ref[0, 0].astype(jnp.bfloat16),
                  preferred_element_type=jnp.float32).reshape(R, TQ, dk)
    imp = _dot_split3(jnp.sum(p_c, axis=0), ov_ref[...])
    selmask = _select_blocks(imp, qpos, n_top).astype(jnp.bfloat16)

    def sel_step(kt, carry):
        k0 = pl.multiple_of(kt * TK, TK)
        kpos = k0 + lax.broadcasted_iota(jnp.int32, (1, TK), 1)
        dist = qpos - kpos
        picked = _dot_nt(selmask, et_ref[pl.ds(k0, TK), :]) > 0.5
        return _flash_step(q, ks_ref[0, 0, pl.ds(k0, TK), :], vs_ref[0, 0, pl.ds(k0, TK), :], slope3,
                           dist.astype(jnp.float32), picked & (dist >= 0), carry)

    n_kt = (s0 + TQ + TK - 1) // TK
    o_s = _flash_out(lax.fori_loop(0, n_kt, sel_step, _flash_init(R, TQ, dk)))

    def win_step(kt, carry):
        k0 = pl.multiple_of(kt * TK, TK)
        kpos = k0 + lax.broadcasted_iota(jnp.int32, (1, TK), 1)
        dist = qpos - kpos
        return _flash_step(q, kw_ref[0, 0, pl.ds(k0, TK), :], vw_ref[0, 0, pl.ds(k0, TK), :], slope3,
                           dist.astype(jnp.float32), (dist >= 0) & (dist <= WINDOW), carry)

    kt_lo = jnp.maximum(s0 - WINDOW, 0) // TK
    o_w = _flash_out(lax.fori_loop(kt_lo, n_kt, win_step, _flash_init(R, TQ, dk)))

    gates = jax.nn.sigmoid(gl_ref[0, 0])
    o_ref[0, 0] = gates[..., 0:1] * o_c + gates[..., 1:2] * o_s + gates[..., 2:3] * o_w


def _nsa_prompt(q, k_cmp, v_cmp, ks, vs, kw, vw, gl):
    B, T, _ = q.shape
    G, R, dk = NSA_GROUPS, NSA_HEADS // NSA_GROUPS, NSA_DK
    TQ, TK = NSA_TQ, NSA_TK
    assert T % TK == 0 and TK % TQ == 0 and TK % SEL_BLOCK == 0
    n_sel = T // SEL_BLOCK
    n_top = min(SEL_TOP, n_sel)
    n_cmp = k_cmp.shape[1]
    n_cmp_pad = _round_up(n_cmp, LANE)
    q5 = q.reshape(B, T, G, R, dk).transpose(0, 2, 3, 1, 4)
    gl5 = gl.reshape(B, T, G, R, 3).transpose(0, 2, 3, 1, 4)
    by_group = lambda t: t.reshape(B, -1, G, dk).transpose(0, 2, 1, 3)
    pad_cmp = lambda t: jnp.pad(by_group(t.reshape(B, n_cmp, G * dk)), ((0, 0), (0, 0), (0, n_cmp_pad - n_cmp), (0, 0)))
    slopes = _alibi_slopes(NSA_HEADS).reshape(G, R, 1, 1)
    cmp_start = np.arange(n_cmp_pad) * CMP_STRIDE
    sel_start = np.arange(n_sel) * SEL_BLOCK
    overlap = ((cmp_start[:, None] < sel_start[None, :] + SEL_BLOCK)
               & (cmp_start[:, None] + CMP_BLOCK - 1 >= sel_start[None, :]) & (np.arange(n_cmp_pad)[:, None] < n_cmp))
    expand_t = (np.arange(T)[:, None] // SEL_BLOCK) == np.arange(n_sel)[None, :]
    row_spec = lambda rows: pl.BlockSpec((1, 1, rows, dk), lambda b, g, t: (b, g, 0, 0))
    out = pl.pallas_call(
        partial(_nsa_prompt_body, n_top=n_top),
        grid=(B, G, T // TQ),
        in_specs=[
            pl.BlockSpec((1, 1, R, TQ, dk), lambda b, g, t: (b, g, 0, t, 0)),
            row_spec(n_cmp_pad), row_spec(n_cmp_pad), row_spec(T), row_spec(T), row_spec(T), row_spec(T),
            pl.BlockSpec((1, 1, R, TQ, 3), lambda b, g, t: (b, g, 0, t, 0)),
            pl.BlockSpec((1, R, 1, 1), lambda b, g, t: (g, 0, 0, 0)),
            pl.BlockSpec((n_cmp_pad, n_sel), lambda b, g, t: (0, 0)),
            pl.BlockSpec((T, n_sel), lambda b, g, t: (0, 0)),
        ],
        out_specs=pl.BlockSpec((1, 1, R, TQ, dk), lambda b, g, t: (b, g, 0, t, 0)),
        out_shape=jax.ShapeDtypeStruct((B, G, R, T, dk), jnp.float32),
        compiler_params=pltpu.CompilerParams(
            dimension_semantics=("arbitrary", "arbitrary", "arbitrary"), vmem_limit_bytes=VMEM_LIMIT),
        name="nsa_prompt",
    )(q5, pad_cmp(k_cmp), pad_cmp(v_cmp), by_group(ks), by_group(vs), by_group(kw), by_group(vw), gl5, slopes,
      jnp.asarray(overlap, jnp.bfloat16), jnp.asarray(expand_t, jnp.bfloat16))
    return out.transpose(0, 3, 1, 2, 4).reshape(B, T, G * R * dk)


FLASH_TK = 256


def _causal_sweep(q, k_ref, v_of, slope3, scale, s0, R, TQ, dv):
    TK = FLASH_TK
    qpos = s0 + lax.broadcasted_iota(jnp.int32, (TQ, 1), 0)

    def step(kt, carry):
        k0 = pl.multiple_of(kt * TK, TK)
        dist = qpos - (k0 + lax.broadcasted_iota(jnp.int32, (1, TK), 1))
        k = k_ref[pl.ds(k0, TK), :]
        return _flash_step(q, k, v_of(k, k0), slope3, dist.astype(jnp.float32), dist >= 0, carry, scale)

    return lax.fori_loop(0, (s0 + TQ + TK - 1) // TK, step, _flash_init(R, TQ, dv))


def _mla_prompt_body(q_ref, kv_ref, o_ref):
    _, H, TQ, C = q_ref.shape
    s0 = pl.program_id(1) * TQ
    q = q_ref[0].reshape(H * TQ, C).astype(jnp.bfloat16)
    carry = _causal_sweep(q, kv_ref.at[0], lambda k, k0: k[:, :MLA_KV_LORA], None,
                          (MLA_NOPE + MLA_ROPE) ** -0.5, s0, H, TQ, MLA_KV_LORA)
    o_ref[0] = _flash_out(carry)


def _mla_prompt(q_cat, kv_new):
    B, T, H, C = q_cat.shape
    TQ = 128
    assert T % FLASH_TK == 0 and FLASH_TK % TQ == 0
    o = pl.pallas_call(
        _mla_prompt_body,
        grid=(B, T // TQ),
        in_specs=[pl.BlockSpec((1, H, TQ, C), lambda b, t: (b, 0, t, 0)),
                  pl.BlockSpec((1, T, C), lambda b, t: (b, 0, 0))],
        out_specs=pl.BlockSpec((1, H, TQ, MLA_KV_LORA), lambda b, t: (b, 0, t, 0)),
        out_shape=jax.ShapeDtypeStruct((B, H, T, MLA_KV_LORA), jnp.float32),
        compiler_params=pltpu.CompilerParams(
            dimension_semantics=("arbitrary", "arbitrary"), vmem_limit_bytes=VMEM_LIMIT),
        name="mla_prompt",
    )(q_cat.transpose(0, 2, 1, 3), kv_new)
    return o.transpose(0, 2, 1, 3)


def _diff_prompt_body(q_ref, k_ref, v_ref, slope_ref, lam_ref, gsub_ref, o_ref, *, out_scale):
    _, _, RC, TQ, d2 = q_ref.shape
    s0 = pl.program_id(2) * TQ
    q = q_ref[0, 0].reshape(RC * TQ, d2).astype(jnp.bfloat16)
    carry = _causal_sweep(q, k_ref.at[0, 0], lambda k, k0: v_ref[0, 0, pl.ds(k0, FLASH_TK), :], slope_ref[0],
                          DIFF_D ** -0.5, s0, RC, TQ, d2)
    pn = _flash_out(carry).reshape(RC // 2, 2, TQ, d2)
    o = pn[:, 0] - lam_ref[...] * pn[:, 1]
    o = o * lax.rsqrt(jnp.mean(o * o, axis=-1, keepdims=True) + EPS)
    o_ref[0, 0] = o * gsub_ref[...] * out_scale


def _diff_prompt(q, k, v, lam, g_sub, lam_init):
    B, T, _ = q.shape
    G, R, d2 = DIFF_KV_HEADS, DIFF_HEADS // DIFF_KV_HEADS, 2 * DIFF_D
    TQ = 256
    assert T % FLASH_TK == 0 and T % TQ == 0
    q6 = q.reshape(B, T, G, R, 2, DIFF_D).transpose(0, 2, 3, 4, 1, 5)
    zero = jnp.zeros_like(q6[:, :, :, 0])
    q_pad = jnp.stack([jnp.concatenate([q6[:, :, :, 0], zero], axis=-1),
                       jnp.concatenate([zero, q6[:, :, :, 1]], axis=-1)], axis=3)
    q_pad = q_pad.reshape(B, G, R * 2, T, d2)
    by_group = lambda t: t.reshape(B, T, G, d2).transpose(0, 2, 1, 3)
    slopes = jnp.repeat(_alibi_slopes(DIFF_HEADS).reshape(G, R), 2, axis=1).reshape(G, R * 2, 1, 1)
    o = pl.pallas_call(
        partial(_diff_prompt_body, out_scale=1.0 - lam_init),
        grid=(B, G, T // TQ),
        in_specs=[pl.BlockSpec((1, 1, R * 2, TQ, d2), lambda b, g, t: (b, g, 0, t, 0)),
                  pl.BlockSpec((1, 1, T, d2), lambda b, g, t: (b, g, 0, 0)),
                  pl.BlockSpec((1, 1, T, d2), lambda b, g, t: (b, g, 0, 0)),
                  pl.BlockSpec((1, R * 2, 1, 1), lambda b, g, t: (g, 0, 0, 0)),
                  pl.BlockSpec((1, 1), lambda b, g, t: (0, 0)),
                  pl.BlockSpec((1, d2), lambda b, g, t: (0, 0))],
        out_specs=pl.BlockSpec((1, 1, R, TQ, d2), lambda b, g, t: (b, g, 0, t, 0)),
        out_shape=jax.ShapeDtypeStruct((B, G, R, T, d2), jnp.float32),
        compiler_params=pltpu.CompilerParams(
            dimension_semantics=("arbitrary", "arbitrary", "arbitrary"), vmem_limit_bytes=VMEM_LIMIT),
        name="diff_prompt",
    )(q_pad, by_group(k), by_group(v), slopes, lam.reshape(1, 1), g_sub.reshape(1, d2))
    return o.transpose(0, 3, 1, 2, 4).reshape(B, T, G * R * d2)


PAGES_PER_STEP = 16


def _page_specs(block, n_pages, tail):
    P = PAGES_PER_STEP
    tail = tuple(tail) + (0,) * (len(block) - 1 - len(tail))

    def index(b, j, pt_ref, *, k):
        return (pt_ref[b * n_pages + j * P + k],) + tail

    return [pl.BlockSpec(block, partial(index, k=k)) for k in range(P)]


def _softmax_update(s, v_list, m_ref, l_ref, acc_ref, vt=False):
    m_prev = m_ref[...]
    m_new = jnp.maximum(m_prev, jnp.max(s, axis=-1, keepdims=True))
    m_safe = jnp.where(m_new == NEG_INF, 0.0, m_new)
    p = jnp.exp(s - m_safe)
    alpha = jnp.exp(m_prev - m_safe)
    l_ref[...] = alpha * l_ref[...] + jnp.sum(p, axis=-1, keepdims=True)
    pv = None
    off = 0
    for v in v_list:
        n = v.shape[1] if vt else v.shape[0]
        pc = p[:, off:off + n].astype(jnp.bfloat16)
        term = (_dot_nt(pc, v.astype(jnp.bfloat16)) if vt
                else jnp.dot(pc, v.astype(jnp.bfloat16), preferred_element_type=jnp.float32))
        pv = term if pv is None else pv + term
        off += n
    acc_ref[...] = alpha * acc_ref[...] + pv
    m_ref[...] = m_new


def _mla_decode_body(pt_ref, q_ref, kvn_ref, *rest, n_tok):
    P = PAGES_PER_STEP
    pages, (o_ref, m_ref, l_ref, acc_ref) = rest[:P], rest[P:]
    j = pl.program_id(1)
    scale = (MLA_NOPE + MLA_ROPE) ** -0.5
    q = q_ref[0].astype(jnp.bfloat16)

    @pl.when(j == 0)
    def _():
        m_ref[...] = jnp.full(m_ref.shape, NEG_INF, jnp.float32)
        l_ref[...] = jnp.zeros(l_ref.shape, jnp.float32)
        acc_ref[...] = jnp.zeros(acc_ref.shape, jnp.float32)

    kvs = [pg[...] for pg in pages]
    s = jnp.concatenate([jnp.dot(q, kv.astype(jnp.bfloat16), preferred_element_type=jnp.float32) for kv in kvs],
                        axis=-1) * scale
    _softmax_update(s, [kv[:MLA_KV_LORA] for kv in kvs], m_ref, l_ref, acc_ref, vt=True)

    @pl.when(j == pl.num_programs(1) - 1)
    def _():
        kvn = kvn_ref[0]
        sn = _dot_nt(q, kvn.astype(jnp.bfloat16)) * scale
        tq = lax.broadcasted_iota(jnp.int32, sn.shape, 0) % n_tok
        tk = lax.broadcasted_iota(jnp.int32, sn.shape, 1)
        _softmax_update(jnp.where(tk <= tq, sn, NEG_INF), [kvn[:, :MLA_KV_LORA]], m_ref, l_ref, acc_ref)
        l = l_ref[...]
        o_ref[0] = acc_ref[...] / jnp.where(l > 0, l, 1.0)


def _mla_decode(q_cat, kv_new, cache_mla, page_table, layer):
    B, T, H, C = q_cat.shape
    n_pages = page_table.shape[1]
    P = PAGES_PER_STEP
    assert n_pages % P == 0
    q_rows = q_cat.transpose(0, 2, 1, 3).reshape(B, H * T, C)
    cache_t = cache_mla.transpose(0, 2, 3, 1)
    grid_spec = pltpu.PrefetchScalarGridSpec(
        num_scalar_prefetch=1,
        grid=(B, n_pages // P),
        in_specs=[pl.BlockSpec((1, H * T, C), lambda b, j, pt: (b, 0, 0)),
                  pl.BlockSpec((1, T, C), lambda b, j, pt: (b, 0, 0))]
        + _page_specs((None, None, C, PAGE_SIZE), n_pages, (layer,)),
        out_specs=pl.BlockSpec((1, H * T, MLA_KV_LORA), lambda b, j, pt: (b, 0, 0)),
        scratch_shapes=[pltpu.VMEM((H * T, 1), jnp.float32), pltpu.VMEM((H * T, 1), jnp.float32),
                        pltpu.VMEM((H * T, MLA_KV_LORA), jnp.float32)],
    )
    o = pl.pallas_call(
        partial(_mla_decode_body, n_tok=T),
        grid_spec=grid_spec,
        out_shape=jax.ShapeDtypeStruct((B, H * T, MLA_KV_LORA), jnp.float32),
        compiler_params=pltpu.CompilerParams(
            dimension_semantics=("arbitrary", "arbitrary"), vmem_limit_bytes=VMEM_LIMIT),
        name="mla_decode",
    )(page_table.reshape(-1), q_rows, kv_new, *([cache_t] * P))
    return o.reshape(B, H, T, MLA_KV_LORA).transpose(0, 2, 1, 3)


def _diff_decode_body(pt_ref, q_ref, kn_ref, vn_ref, slope_ref, lam_ref, gsub_ref, *rest, n_tok, pos0, out_scale,
                      layer):
    P = PAGES_PER_STEP
    pages, (o_ref, m_ref, l_ref, acc_ref) = rest[:P], rest[P:]
    j = pl.program_id(1)
    G = q_ref.shape[1]
    per_tok = pages[0].shape[0] // PAGE_SIZE
    k_rows = lambda pg, g: pg[pl.ds(layer * 2 * G + g, PAGE_SIZE, stride=per_tok), :]
    v_rows = lambda pg, g: pg[pl.ds(layer * 2 * G + G + g, PAGE_SIZE, stride=per_tok), :]
    rows = q_ref.shape[2]
    tq = lax.broadcasted_iota(jnp.int32, (rows, 1), 0) % n_tok

    @pl.when(j == 0)
    def _():
        m_ref[...] = jnp.full(m_ref.shape, NEG_INF, jnp.float32)
        l_ref[...] = jnp.zeros(l_ref.shape, jnp.float32)
        acc_ref[...] = jnp.zeros(acc_ref.shape, jnp.float32)

    for g in range(G):
        q = q_ref[0, g].astype(jnp.bfloat16)
        slope = slope_ref[g]
        kpos = (j * P * PAGE_SIZE
                + lax.broadcasted_iota(jnp.int32, (1, P * PAGE_SIZE), 1))
        dist = (pos0 + tq - kpos).astype(jnp.float32)
        s = jnp.concatenate([_dot_nt(q, k_rows(pg, g).astype(jnp.bfloat16)) for pg in pages], axis=-1)
        s = s * (DIFF_D ** -0.5) - slope * dist
        _softmax_update(s, [v_rows(pg, g) for pg in pages], m_ref.at[g], l_ref.at[g], acc_ref.at[g])

    @pl.when(j == pl.num_programs(1) - 1)
    def _():
        lam = lam_ref[...]
        for g in range(G):
            q = q_ref[0, g].astype(jnp.bfloat16)
            sn = _dot_nt(q, kn_ref[0, g].astype(jnp.bfloat16)) * (DIFF_D ** -0.5)
            dist = tq - lax.broadcasted_iota(jnp.int32, sn.shape, 1)
            sn = jnp.where(dist >= 0, sn - slope_ref[g] * dist.astype(jnp.float32), NEG_INF)
            _softmax_update(sn, [vn_ref[0, g]], m_ref.at[g], l_ref.at[g], acc_ref.at[g])
            l = l_ref[g]
            pn = acc_ref[g] / jnp.where(l > 0, l, 1.0)
            R = rows // (2 * n_tok)
            pn = pn.reshape(R, 2, n_tok, pn.shape[-1])
            o = (pn[:, 0] - lam * pn[:, 1]).reshape(R * n_tok, pn.shape[-1])
            o = o * lax.rsqrt(jnp.mean(o * o, axis=-1, keepdims=True) + EPS)
            o_ref[0, g] = o * gsub_ref[...] * out_scale


def _diff_decode(q, k, v, cache_diff, page_table, layer, lam, g_sub, lam_init):
    B, T, _ = q.shape
    G, R, d2 = DIFF_KV_HEADS, DIFF_HEADS // DIFF_KV_HEADS, 2 * DIFF_D
    n_pages = page_table.shape[1]
    P = PAGES_PER_STEP
    assert n_pages % P == 0
    rows = R * 2 * T
    q6 = q.reshape(B, T, G, R, 2, DIFF_D).transpose(0, 2, 3, 4, 1, 5)
    zero = jnp.zeros_like(q6[:, :, :, 0])
    q_pad = jnp.stack([jnp.concatenate([q6[:, :, :, 0], zero], axis=-1),
                       jnp.concatenate([zero, q6[:, :, :, 1]], axis=-1)], axis=3)
    q_pad = q_pad.reshape(B, G, rows, d2)
    by_group = lambda t: t.reshape(B, T, G, d2).transpose(0, 2, 1, 3)
    slopes = jnp.broadcast_to(_alibi_slopes(DIFF_HEADS).reshape(G, R, 1, 1), (G, R, 2 * T, 1)).reshape(G, rows, 1)
    n_layers = cache_diff.shape[2]
    cache_t = cache_diff.reshape(cache_diff.shape[0], PAGE_SIZE * n_layers * 2 * G, d2)
    grid_spec = pltpu.PrefetchScalarGridSpec(
        num_scalar_prefetch=1,
        grid=(B, n_pages // P),
        in_specs=[pl.BlockSpec((1, G, rows, d2), lambda b, j, pt: (b, 0, 0, 0)),
                  pl.BlockSpec((1, G, T, d2), lambda b, j, pt: (b, 0, 0, 0)),
                  pl.BlockSpec((1, G, T, d2), lambda b, j, pt: (b, 0, 0, 0)),
                  pl.BlockSpec((G, rows, 1), lambda b, j, pt: (0, 0, 0)),
                  pl.BlockSpec((1, 1), lambda b, j, pt: (0, 0)),
                  pl.BlockSpec((1, d2), lambda b, j, pt: (0, 0))]
        + _page_specs((None, PAGE_SIZE * n_layers * 2 * G, d2), n_pages, ()),
        out_specs=pl.BlockSpec((1, G, R * T, d2), lambda b, j, pt: (b, 0, 0, 0)),
        scratch_shapes=[pltpu.VMEM((G, rows, 1), jnp.float32), pltpu.VMEM((G, rows, 1), jnp.float32),
                        pltpu.VMEM((G, rows, d2), jnp.float32)],
    )
    o = pl.pallas_call(
        partial(_diff_decode_body, n_tok=T, pos0=n_pages * PAGE_SIZE, out_scale=1.0 - lam_init, layer=layer),
        grid_spec=grid_spec,
        out_shape=jax.ShapeDtypeStruct((B, G, R * T, d2), jnp.float32),
        compiler_params=pltpu.CompilerParams(
            dimension_semantics=("arbitrary", "arbitrary"), vmem_limit_bytes=VMEM_LIMIT),
        name="diff_decode",
    )(page_table.reshape(-1), q_pad, by_group(k), by_group(v), slopes, lam.reshape(1, 1), g_sub.reshape(1, d2),
      *([cache_t] * P))
    return o.reshape(B, G, R, T, d2).transpose(0, 3, 1, 2, 4).reshape(B, T, G * R * d2)


CMP_PAGES_PER_STEP = 32


def _cmp_proj_body(x_ref, perm_ref, pe_ref, w_ref, o_ref, pa_ref):
    PB, _, G, dk, page = x_ref.shape
    n_str = page // CMP_STRIDE
    perm = perm_ref[...]

    def regroup(p, carry):
        for kind in range(2):
            for g in range(G):
                pa_ref[kind, g, p] = _dot_nt(perm, x_ref[p, kind, g].astype(jnp.bfloat16))
        return carry

    lax.fori_loop(0, PB, regroup, 0, unroll=4)
    for kind in range(2):
        for g in range(G):
            acc = [jnp.zeros((PB * n_str, CMP_HIDDEN), jnp.float32) for _ in range(2)]
            for k in range(CMP_STRIDE // 2):
                a = pa_ref[kind, g, :, (2 * k) * n_str:(2 * k + 1) * n_str, :].reshape(PB * n_str, dk)
                b = pa_ref[kind, g, :, (2 * k + 1) * n_str:(2 * k + 2) * n_str, :].reshape(PB * n_str, dk)
                v = jnp.concatenate([a, b], axis=-1)
                for half in range(2):
                    acc[half] = acc[half] + jnp.dot((v + pe_ref[kind, half, k]).astype(jnp.bfloat16),
                                                    w_ref[kind, half, k], preferred_element_type=jnp.float32)
            o_ref[kind, g] = jnp.concatenate(acc, axis=-1)


def _cmp_proj(cache_nsa, layer, pe, w1):
    POOL = cache_nsa.shape[0]
    G, dk, J = NSA_GROUPS, NSA_DK, CMP_STRIDE
    n_str = PAGE_SIZE // J
    PB = CMP_PAGES_PER_STEP
    assert POOL % PB == 0 and CMP_BLOCK == 2 * J and J % 2 == 0
    xt = cache_nsa.transpose(0, 2, 3, 4, 5, 1)
    row = np.arange(PAGE_SIZE)
    perm = (row[None, :] == (row[:, None] % n_str) * J + row[:, None] // n_str)
    pe_f = pe.reshape(2, 2, J // 2, 1, 2 * dk)
    w_f = w1.reshape(2, 2, J // 2, 2 * dk, CMP_HIDDEN).astype(jnp.bfloat16)
    const = lambda shape: pl.BlockSpec(shape, lambda i: (0,) * len(shape))
    out = pl.pallas_call(
        _cmp_proj_body,
        grid=(POOL // PB,),
        in_specs=[pl.BlockSpec((PB, None, 2, G, dk, PAGE_SIZE), lambda i: (i, layer, 0, 0, 0, 0)),
                  const((PAGE_SIZE, PAGE_SIZE)), const(pe_f.shape), const(w_f.shape)],
        out_specs=pl.BlockSpec((2, G, PB * n_str, 2 * CMP_HIDDEN), lambda i: (0, 0, i, 0)),
        out_shape=jax.ShapeDtypeStruct((2, G, POOL * n_str, 2 * CMP_HIDDEN), jnp.float32),
        scratch_shapes=[pltpu.VMEM((2, G, PB, PAGE_SIZE, dk), jnp.float32)],
        compiler_params=pltpu.CompilerParams(dimension_semantics=("arbitrary",), vmem_limit_bytes=VMEM_LIMIT),
        name="nsa_cmp_proj",
    )(xt, jnp.asarray(perm, jnp.bfloat16), pe_f, w_f)
    return out.reshape(2, G, POOL, n_str, 2 * CMP_HIDDEN)


def _nsa_decode_body(pt_ref, q_ref, lohi_ref, b1_ref, w2_ref, rn_ref, wp_ref, wn_ref, gl_ref, slope_ref, ov_ref,
                     et_ref, *rest, n_tok, pos0, n_sel, n_top):
    P = PAGES_PER_STEP
    pages = rest[:P]
    o_ref, sel_ref, oc_ref, m_ref, l_ref, acc_ref, mw_ref, lw_ref, accw_ref = rest[P:]
    j = pl.program_id(1)
    G, rows, dk = q_ref.shape[1:]
    T = n_tok
    R = rows // T
    scale = NSA_DK ** -0.5
    tq = lax.broadcasted_iota(jnp.int32, (T, 1), 0)
    qpos = pos0 + tq

    @pl.when(j == 0)
    def _():
        for ref in (m_ref, mw_ref):
            ref[...] = jnp.full(ref.shape, NEG_INF, jnp.float32)
        for ref in (l_ref, acc_ref, lw_ref, accw_ref):
            ref[...] = jnp.zeros(ref.shape, jnp.float32)
        n_cmp = lohi_ref.shape[3]
        cmp_end = lax.broadcasted_iota(jnp.int32, (1, n_cmp), 1) * CMP_STRIDE + (CMP_BLOCK - 1)
        dist_c = qpos - cmp_end
        for g in range(G):
            kv_cmp = []
            for kind in range(2):
                lohi = lohi_ref[kind, g, 0]
                lo, hi = lohi[:, :CMP_HIDDEN], lohi[:, CMP_HIDDEN:]
                hi_next = pltpu.roll(hi, n_cmp - 1, 0)
                pre = lo + hi_next + b1_ref[kind]
                h = pre * jax.nn.sigmoid(pre)
                kv_cmp.append(jnp.dot(h.astype(jnp.bfloat16), w2_ref[kind], preferred_element_type=jnp.float32))
            k_cmp, v_cmp = kv_cmp
            q = (q_ref[0, g] * scale).astype(jnp.bfloat16)
            s = _dot_nt(q, k_cmp.astype(jnp.bfloat16))
            s3 = s.reshape(R, T, n_cmp) - slope_ref[g] * dist_c.astype(jnp.float32)[None]
            s3 = jnp.where((dist_c >= 0)[None], s3, NEG_INF)
            m_c = jnp.max(s3, axis=-1, keepdims=True)
            e_c = jnp.exp(s3 - jnp.where(m_c == NEG_INF, 0.0, m_c))
            d_c = jnp.sum(e_c, axis=-1, keepdims=True)
            p_c = e_c / jnp.where(d_c > 0, d_c, 1.0)
            oc_ref[g] = jnp.dot(p_c.reshape(rows, n_cmp).astype(jnp.bfloat16), v_cmp.astype(jnp.bfloat16),
                                preferred_element_type=jnp.float32)
            imp = _dot_split3(jnp.sum(p_c, axis=0), ov_ref[...])
            sel_ref[g] = _select_blocks(imp, qpos, n_top, n_sel)

    n_keys = P * PAGE_SIZE
    kpos = j * n_keys + lax.broadcasted_iota(jnp.int32, (1, n_keys), 1)
    dist = (qpos - kpos).astype(jnp.float32)
    for g in range(G):
        q = (q_ref[0, g] * scale).astype(jnp.bfloat16)
        picked = _dot_nt(sel_ref[g].astype(jnp.bfloat16), et_ref[...]) > 0.5
        s = jnp.concatenate([jnp.dot(q, pg[0, g].astype(jnp.bfloat16), preferred_element_type=jnp.float32)
                             for pg in pages], axis=-1)
        s3 = s.reshape(R, T, n_keys) - slope_ref[g] * dist[None]
        s3 = jnp.where(picked[None], s3, NEG_INF)
        _softmax_update(s3.reshape(rows, n_keys), [pg[1, g] for pg in pages], m_ref.at[g], l_ref.at[g], acc_ref.at[g],
                        vt=True)

    @pl.when(j == pl.num_programs(1) - 1)
    def _():
        tk = lax.broadcasted_iota(jnp.int32, (1, T), 1)
        dist_n = tq - tk
        n_win = wp_ref.shape[4]
        dist_p = n_win + tq - lax.broadcasted_iota(jnp.int32, (1, n_win), 1)
        gates = jax.nn.sigmoid(gl_ref[0])
        for g in range(G):
            q = (q_ref[0, g] * scale).astype(jnp.bfloat16)
            slope3 = slope_ref[g]

            def scores(k, d, mask, kt=False):
                k = k.astype(jnp.bfloat16)
                s = jnp.dot(q, k, preferred_element_type=jnp.float32) if kt else _dot_nt(q, k)
                s3 = s.reshape(R, T, s.shape[-1]) - slope3 * d.astype(jnp.float32)[None]
                return jnp.where(mask[None], s3, NEG_INF).reshape(rows, s.shape[-1])

            cur = pos0 // SEL_BLOCK
            cur_picked = sel_ref[g][:, cur:cur + 1] > 0.5
            _softmax_update(scores(rn_ref[0, 2, g], dist_n, (dist_n >= 0) & cur_picked), [rn_ref[0, 3, g]],
                            m_ref.at[g], l_ref.at[g], acc_ref.at[g])
            _softmax_update(scores(wp_ref[0, 0, g], dist_p, (dist_p >= 0) & (dist_p <= WINDOW), kt=True),
                            [wp_ref[0, 1, g]], mw_ref.at[g], lw_ref.at[g], accw_ref.at[g], vt=True)
            _softmax_update(scores(wn_ref[0, 0, g], dist_n, dist_n >= 0), [wn_ref[0, 1, g]],
                            mw_ref.at[g], lw_ref.at[g], accw_ref.at[g])
            l_s, l_w = l_ref[g], lw_ref[g]
            o_s = acc_ref[g] / jnp.where(l_s > 0, l_s, 1.0)
            o_w = accw_ref[g] / jnp.where(l_w > 0, l_w, 1.0)
            gt = gates[g]
            o_ref[0, g] = gt[:, 0:1] * oc_ref[g] + gt[:, 1:2] * o_s + gt[:, 2:3] * o_w


def _nsa_decode(q, rows_new, win_new, gl, cache_nsa, state_win, page_table, layer, pe, w1, b1, w2):
    B, T, _ = q.shape
    G, R, dk = NSA_GROUPS, NSA_HEADS // NSA_GROUPS, NSA_DK
    n_pages = page_table.shape[1]
    P = PAGES_PER_STEP
    pos0 = n_pages * PAGE_SIZE
    S = pos0 + T
    assert n_pages % P == 0 and T <= CMP_STRIDE and T <= SEL_BLOCK and pos0 % SEL_BLOCK == 0
    assert state_win.shape[2] == WINDOW
    n_str = PAGE_SIZE // CMP_STRIDE
    n_cmp = n_pages * n_str
    n_sel = -(-S // SEL_BLOCK)
    n_sel_pad = _round_up(n_sel, LANE)
    n_top = min(SEL_TOP, n_sel)
    rows = R * T
    lohi = _cmp_proj(cache_nsa, layer, pe, w1)
    lohi = lohi[:, :, page_table].reshape(2, G, B, n_cmp, 2 * CMP_HIDDEN)
    q_rows = q.reshape(B, T, G, R, dk).transpose(0, 2, 3, 1, 4).reshape(B, G, rows, dk)
    gl_rows = gl.reshape(B, T, G, R, 3).transpose(0, 2, 3, 1, 4).reshape(B, G, rows, 3)
    slopes = _alibi_slopes(NSA_HEADS).reshape(G, R, 1, 1)
    cmp_start = np.arange(n_cmp) * CMP_STRIDE
    sel_start = np.arange(n_sel_pad) * SEL_BLOCK
    overlap = ((cmp_start[:, None] < sel_start[None, :] + SEL_BLOCK)
               & (cmp_start[:, None] + CMP_BLOCK - 1 >= sel_start[None, :]) & (np.arange(n_cmp)[:, None] < n_cmp - 1)
               & (np.arange(n_sel_pad)[None, :] < n_sel))
    expand_t = (jnp.arange(pos0)[:, None] // SEL_BLOCK) == jnp.arange(n_sel_pad)[None, :]
    cache_t = cache_nsa.transpose(0, 2, 3, 4, 5, 1)
    win_t = state_win.transpose(0, 1, 3, 4, 5, 2)
    sel_pages = _page_specs((None, None, 2, G, dk, PAGE_SIZE), n_pages, (layer, 1))
    per_b = lambda shape: pl.BlockSpec((1,) + shape, lambda b, j, pt: (b,) + (0,) * len(shape))
    const = lambda shape: pl.BlockSpec(shape, lambda b, j, pt: (0,) * len(shape))
    grid_spec = pltpu.PrefetchScalarGridSpec(
        num_scalar_prefetch=1,
        grid=(B, n_pages // P),
        in_specs=[per_b((G, rows, dk)),
                  pl.BlockSpec((2, G, 1, n_cmp, 2 * CMP_HIDDEN), lambda b, j, pt: (0, 0, b, 0, 0)),
                  const((2, 1, CMP_HIDDEN)), const((2, CMP_HIDDEN, dk)),
                  per_b((4, G, T, dk)),
                  pl.BlockSpec((1, None, 2, G, dk, WINDOW), lambda b, j, pt: (b, layer, 0, 0, 0, 0)),
                  per_b((2, G, T, dk)), per_b((G, rows, 3)),
                  const((G, R, 1, 1)), const((n_cmp, n_sel_pad)),
                  pl.BlockSpec((P * PAGE_SIZE, n_sel_pad), lambda b, j, pt: (j, 0))]
        + sel_pages,
        out_specs=per_b((G, rows, dk)),
        scratch_shapes=[pltpu.VMEM((G, T, n_sel_pad), jnp.float32), pltpu.VMEM((G, rows, dk), jnp.float32),
                        pltpu.VMEM((G, rows, 1), jnp.float32), pltpu.VMEM((G, rows, 1), jnp.float32),
                        pltpu.VMEM((G, rows, dk), jnp.float32),
                        pltpu.VMEM((G, rows, 1), jnp.float32), pltpu.VMEM((G, rows, 1), jnp.float32),
                        pltpu.VMEM((G, rows, dk), jnp.float32)],
    )
    o = pl.pallas_call(
        partial(_nsa_decode_body, n_tok=T, pos0=pos0, n_sel=n_sel, n_top=n_top),
        grid_spec=grid_spec,
        out_shape=jax.ShapeDtypeStruct((B, G, rows, dk), jnp.float32),
        compiler_params=pltpu.CompilerParams(
            dimension_semantics=("arbitrary", "arbitrary"), vmem_limit_bytes=VMEM_LIMIT),
        name="nsa_decode",
    )(page_table.reshape(-1), q_rows, lohi, b1.reshape(2, 1, CMP_HIDDEN), w2.astype(jnp.bfloat16),
      rows_new.transpose(0, 2, 3, 1, 4), win_t, win_new.transpose(0, 2, 3, 1, 4),
      gl_rows, slopes, jnp.asarray(overlap, jnp.bfloat16), expand_t.astype(jnp.bfloat16),
      *([cache_t] * P))
    return o.reshape(B, G, R, T, dk).transpose(0, 3, 1, 2, 4).reshape(B, T, G * R * dk)


def _rms(x, g):
    xf = x.astype(jnp.float32)
    y = xf * lax.rsqrt(jnp.mean(xf * xf, axis=-1, keepdims=True) + EPS)
    return (y * g.astype(jnp.float32)).astype(x.dtype)


def _split(x, sizes):
    return jnp.split(x, np.cumsum(sizes)[:-1].tolist(), axis=-1)


def _alibi_slopes(n):
    return 2.0 ** (-8.0 * jnp.arange(1, n + 1, dtype=jnp.float32) / n)


def _rope(x, pos):
    half = x.shape[-1] // 2
    freq = ROPE_THETA ** (-jnp.arange(half, dtype=jnp.float32) / half)
    ang = pos.astype(jnp.float32)[:, None] * freq[None, :]
    shape = (1, ang.shape[0]) + (1,) * (x.ndim - 3) + (half,)
    cos, sin = jnp.cos(ang).reshape(shape), jnp.sin(ang).reshape(shape)
    xf = x.astype(jnp.float32)
    x1, x2 = xf[..., :half], xf[..., half:]
    return jnp.concatenate([x1 * cos - x2 * sin, x2 * cos + x1 * sin], axis=-1).astype(x.dtype)


def _masked_softmax(s, mask):
    s = jnp.where(mask, s, -jnp.inf)
    m = jnp.max(s, axis=-1, keepdims=True)
    m = jnp.where(jnp.isfinite(m), m, 0.0)
    e = jnp.exp(s - m)
    d = jnp.sum(e, axis=-1, keepdims=True)
    return e / jnp.where(d > 0, d, 1.0)


def _sweep(fn, n_q, block=Q_BLOCK):
    qb = block if n_q % block == 0 else n_q
    nb = n_q // qb
    if nb == 1:
        return fn(0, qb)
    out = lax.map(lambda i: fn(i * qb, qb), jnp.arange(nb))
    out = jnp.moveaxis(out, 0, 1)
    return out.reshape((out.shape[0], nb * qb) + out.shape[3:])


def _gather_pages(cache, page_table, layer):
    g = cache[page_table, :, layer]
    return g.reshape((g.shape[0], g.shape[1] * g.shape[2]) + g.shape[3:])


def _mla(cq, ckv, kr, pos, kv_past, g_q, w_uq, g_kv, w_uk, w_uv, paged=None):
    q = jnp.einsum('btc,chd->bthd', _rms(cq, g_q), w_uq)
    q_nope, q_rope = q[..., :MLA_NOPE], _rope(q[..., MLA_NOPE:], pos)
    kv_new = jnp.concatenate([_rms(ckv, g_kv), _rope(kr, pos)], axis=-1)
    q_cat = jnp.concatenate([jnp.einsum('bthd,rhd->bthr', q_nope, w_uk), q_rope], axis=-1)
    if paged is not None or kv_past is None:
        o_lat = _mla_decode(q_cat, kv_new, *paged) if paged is not None else _mla_prompt(q_cat, kv_new)
        o = jnp.einsum('bthr,rhv->bthv', o_lat, w_uv)
        return o.reshape(o.shape[:2] + (-1,)), kv_new
    kv_all = kv_new if kv_past is None else jnp.concatenate([kv_past, kv_new], axis=1)
    lat = kv_all[..., :MLA_KV_LORA]
    k_pos = jnp.arange(kv_all.shape[1])
    scale = (MLA_NOPE + MLA_ROPE) ** -0.5

    def block(s0, qb):
        qc = lax.dynamic_slice_in_dim(q_cat, s0, qb, axis=1)
        qp = lax.dynamic_slice_in_dim(pos, s0, qb)
        s = jnp.einsum('bqhc,bkc->bhqk', qc, kv_all).astype(jnp.float32) * scale
        p = _masked_softmax(s, k_pos[None, :] <= qp[:, None])
        return jnp.einsum('bhqk,bkr->bqhr', p.astype(lat.dtype), lat)

    o_lat = _sweep(block, q_cat.shape[1])
    o = jnp.einsum('bthr,rhv->bthv', o_lat, w_uv)
    return o.reshape(o.shape[:2] + (-1,)), kv_new


def _diff(q, k, v, pos, kv_past, lam_p, g_sub, lam_init, paged=None):
    B, T = q.shape[:2]
    G, R = DIFF_KV_HEADS, DIFF_HEADS // DIFF_KV_HEADS
    kv_new = jnp.stack([k.reshape(B, T, G, 2 * DIFF_D), v.reshape(B, T, G, 2 * DIFF_D)], axis=2)
    lp = lam_p.astype(jnp.float32)
    lam = jnp.exp(jnp.sum(lp[0] * lp[1])) - jnp.exp(jnp.sum(lp[2] * lp[3])) + lam_init
    if paged is not None:
        return _diff_decode(q, k, v, *paged, lam, g_sub, lam_init), kv_new
    if kv_past is None:
        return _diff_prompt(q, k, v, lam, g_sub, lam_init), kv_new
    q = q.reshape(B, T, G, R, 2, DIFF_D)
    kv_all = kv_new if kv_past is None else jnp.concatenate([kv_past, kv_new], axis=1)
    Tk = kv_all.shape[1]
    k_all = kv_all[:, :, 0].reshape(B, Tk, G, 2, DIFF_D)
    v_all = kv_all[:, :, 1]
    k_pos = jnp.arange(Tk)
    slopes = _alibi_slopes(DIFF_HEADS).reshape(1, G, R, 1, 1, 1)

    def block(s0, qb):
        qc = lax.dynamic_slice_in_dim(q, s0, qb, axis=1)
        qp = lax.dynamic_slice_in_dim(pos, s0, qb)
        dist = qp[:, None] - k_pos[None, :]
        s = jnp.einsum('bqgrcd,bkgcd->bgrcqk', qc, k_all).astype(jnp.float32) * (DIFF_D ** -0.5)
        p = _masked_softmax(s - slopes * dist.astype(jnp.float32), dist >= 0)
        a = p[:, :, :, 0] - lam * p[:, :, :, 1]
        return jnp.einsum('bgrqk,bkge->bqgre', a.astype(v_all.dtype), v_all)

    o = _sweep(block, T)
    o = _rms(o, g_sub) * (1.0 - lam_init)
    return o.reshape(B, T, -1), kv_new


def _compress(x, pe, w1, b1, w2):
    B, S, G, dk = x.shape
    nc = S // CMP_STRIDE
    xc = x[:, :nc * CMP_STRIDE].reshape(B, nc, CMP_STRIDE, G, dk)
    lo = jnp.einsum('bnjgd,jdh->bngh', xc + pe[:CMP_STRIDE, None, :], w1[:CMP_STRIDE])
    hi = jnp.einsum('bnjgd,jdh->bngh', xc + pe[CMP_STRIDE:, None, :], w1[CMP_STRIDE:])
    h = jax.nn.silu(lo[:, :-1] + hi[:, 1:] + b1)
    return jnp.einsum('bngh,hd->bngd', h, w2)


def _nsa(q, rows_new, win_new, gate_logits, pos, pos0, rows_past, win_past, pe, w1, b1, w2):
    B, T = q.shape[:2]
    G, R = NSA_GROUPS, NSA_HEADS // NSA_GROUPS
    q = q.reshape(B, T, G, R, NSA_DK)
    gates = jax.nn.sigmoid(gate_logits.astype(jnp.float32)).reshape(B, T, G, R, 3)
    rows = rows_new if rows_past is None else jnp.concatenate([rows_past, rows_new], axis=1)
    S = rows.shape[1]
    if win_past is None:
        win = jnp.pad(win_new, ((0, 0), (WINDOW, 0), (0, 0), (0, 0), (0, 0)))
        win_pos0 = -WINDOW
        win_state = win_new[:, T - min(WINDOW, T):]
    else:
        win = jnp.concatenate([win_past, win_new], axis=1)
        win_pos0 = pos0 - win_past.shape[1]
        win_state = win[:, T:]
    span = win.shape[1] - T
    k_cmp = _compress(rows[:, :, 0], pe[0], w1[0], b1[0], w2[0])
    v_cmp = _compress(rows[:, :, 1], pe[1], w1[1], b1[1], w2[1])
    n_cmp = k_cmp.shape[1]
    cmp_start = jnp.arange(n_cmp) * CMP_STRIDE
    cmp_end = cmp_start + CMP_BLOCK - 1
    n_sel = -(-S // SEL_BLOCK)
    sel_start = jnp.arange(n_sel) * SEL_BLOCK
    overlap = ((cmp_start[:, None] < sel_start[None, :] + SEL_BLOCK)
               & (cmp_end[:, None] >= sel_start[None, :])).astype(jnp.float32)
    sel = jnp.pad(rows[:, :, 2:], ((0, 0), (0, n_sel * SEL_BLOCK - S), (0, 0), (0, 0), (0, 0)))
    sel = sel.reshape(B, n_sel, SEL_BLOCK, 2, G, NSA_DK).transpose(0, 4, 3, 1, 2, 5)
    n_top = min(SEL_TOP, n_sel)
    slopes = _alibi_slopes(NSA_HEADS).reshape(G, R)
    scale = NSA_DK ** -0.5
    take_blocks = jax.vmap(jax.vmap(lambda blk, ix: blk[:, ix]))
    j_sel = jnp.arange(n_sel)

    def block(s0, qb):
        qc = lax.dynamic_slice_in_dim(q, s0, qb, axis=1)
        qp = lax.dynamic_slice_in_dim(pos, s0, qb)
        gc = lax.dynamic_slice_in_dim(gates, s0, qb, axis=1)
        dist_c = qp[:, None] - cmp_end[None, :]
        s = jnp.einsum('bqgrd,bngd->bgrqn', qc, k_cmp).astype(jnp.float32) * scale
        p_c = _masked_softmax(s - slopes[None, :, :, None, None] * dist_c.astype(jnp.float32), dist_c >= 0)
        o_c = jnp.einsum('bgrqn,bngd->bqgrd', p_c.astype(v_cmp.dtype), v_cmp)
        imp = jnp.einsum('bgrqn,ns->bgqs', p_c, overlap)
        cur = qp // SEL_BLOCK
        valid = sel_start[None, :] <= qp[:, None]
        forced = (j_sel[None, :] == 0) | (j_sel[None, :] == cur[:, None]) | (j_sel[None, :] == cur[:, None] - 1)
        score = jnp.where(valid, jnp.where(forced, jnp.inf, imp), -jnp.inf)
        _, idx = lax.top_k(score, n_top)
        kv_sel = take_blocks(sel, idx)
        kpos = idx[..., None] * SEL_BLOCK + jnp.arange(SEL_BLOCK)
        dist_s = qp[None, None, :, None, None] - kpos
        s = jnp.einsum('bqgrd,bgqnld->bgrqnl', qc, kv_sel[:, :, 0]).astype(jnp.float32) * scale
        s = s - slopes[None, :, :, None, None, None] * dist_s[:, :, None].astype(jnp.float32)
        sh = s.shape
        p_s = _masked_softmax(s.reshape(sh[:4] + (-1,)),
                              (dist_s >= 0)[:, :, None].reshape(sh[0], sh[1], 1, sh[3], -1)).reshape(sh)
        o_s = jnp.einsum('bgrqnl,bgqnld->bqgrd', p_s.astype(kv_sel.dtype), kv_sel[:, :, 1])
        w = lax.dynamic_slice_in_dim(win, s0, span + qb, axis=1)
        wpos = win_pos0 + s0 + jnp.arange(span + qb)
        dist_w = qp[:, None] - wpos[None, :]
        mask_w = (wpos[None, :] >= 0) & (dist_w >= 0) & (dist_w <= WINDOW)
        s = jnp.einsum('bqgrd,bkgd->bgrqk', qc, w[:, :, 0]).astype(jnp.float32) * scale
        p_w = _masked_softmax(s - slopes[None, :, :, None, None] * dist_w.astype(jnp.float32), mask_w)
        o_w = jnp.einsum('bgrqk,bkgd->bqgrd', p_w.astype(w.dtype), w[:, :, 1])
        return (gc[..., 0:1] * o_c + gc[..., 1:2] * o_s + gc[..., 2:3] * o_w).astype(q.dtype)

    o = _sweep(block, T, SEL_Q_BLOCK)
    return o.reshape(B, T, -1), win_state


def _pad_cols(w, n):
    return jnp.pad(w, ((0, 0), (0, n - w.shape[1])))


def _run(x, c, p, cache_mla=None, cache_diff=None, cache_nsa=None, state_win=None, page_table=None):
    B, T, _ = x.shape
    has_past = page_table is not None
    pos0 = page_table.shape[1] * PAGE_SIZE if has_past else 0
    pos = pos0 + jnp.arange(T)
    mod = jnp.einsum('bc,lcd->lbd', jax.nn.silu(c), p['ada_w']) + p['ada_b'][:, None, :]
    mla_rows, diff_rows, nsa_rows, win_rows = [], [], [], []
    h = x
    n_ab, n_c = sum(AB_SPLIT), sum(C_SPLIT)
    for l in range(DEPTH):
        shift, scale, gate = jnp.split(mod[l][:, None, :], 3, axis=-1)
        pre_g = p['pre_g'][l].reshape(1, 1, D_MODEL)
        post_g = p['post_g'][l].reshape(1, 1, D_MODEL)
        i = l // 2
        if l % 2 == 0:
            w_in = _pad_cols(p['ab_w_in'][i], _round_up(n_ab, LANE)).astype(jnp.bfloat16)
            proj = _inproj(h, scale, shift, pre_g, w_in)[..., :n_ab]
            cq, ckv, kr, ga, qd, kd, vd, gb = _split(proj, AB_SPLIT)
            o_a, kv_a = _mla(cq, ckv, kr, pos, None,
                             p['mla_q_g'][i], p['mla_w_uq'][i], p['mla_kv_g'][i], p['mla_w_uk'][i], p['mla_w_uv'][i],
                             paged=(cache_mla, page_table, i) if has_past else None)
            o_b, kv_b = _diff(qd, kd, vd, pos, None,
                              p['diff_lam'][i], p['diff_sub_g'][i], 0.8 - 0.6 * math.exp(-0.3 * l),
                              paged=(cache_diff, page_table, i) if has_past else None)
            o = jnp.concatenate([o_a, o_b], axis=-1)
            gsrc = jnp.concatenate([ga, gb], axis=-1)
            h = _outproj(o, gsrc, h, gate, post_g, p['ab_w_out'][i].astype(jnp.bfloat16))
            mla_rows.append(kv_a)
            diff_rows.append(kv_b)
        else:
            w_in = _pad_cols(p['c_w_in'][i], _round_up(n_c, LANE)).astype(jnp.bfloat16)
            proj = _inproj(h, scale, shift, pre_g, w_in)[..., :n_c]
            qn, kc, vc, ks, vs, kw, vw, gl, gn = _split(proj, C_SPLIT)
            grp = lambda t: t.reshape(B, T, NSA_GROUPS, NSA_DK)
            rows_new = jnp.stack([grp(kc), grp(vc), grp(ks), grp(vs)], axis=2)
            win_new = jnp.stack([grp(kw), grp(vw)], axis=2)
            if has_past:
                o_n = _nsa_decode(qn, rows_new, win_new, gl, cache_nsa, state_win, page_table, i,
                                  p['nsa_pe'][i], p['nsa_w1'][i], p['nsa_b1'][i], p['nsa_w2'][i])
                win_state = jnp.concatenate([state_win[:, i], win_new], axis=1)[:, T:]
            else:
                pe, w1, b1, w2 = p['nsa_pe'][i], p['nsa_w1'][i], p['nsa_b1'][i], p['nsa_w2'][i]
                k_cmp = _compress(grp(kc), pe[0], w1[0], b1[0], w2[0])
                v_cmp = _compress(grp(vc), pe[1], w1[1], b1[1], w2[1])
                o_n = _nsa_prompt(qn, k_cmp, v_cmp, ks, vs, kw, vw, gl)
                win_state = win_new[:, T - min(WINDOW, T):]
            h = _outproj(o_n, gn, h, gate, post_g, p['c_w_out'][i].astype(jnp.bfloat16))
            nsa_rows.append(rows_new)
            win_rows.append(win_state)
    return (h, jnp.stack(mla_rows, axis=2), jnp.stack(diff_rows, axis=2),
            jnp.stack(nsa_rows, axis=2), jnp.stack(win_rows, axis=1))


def kernel(x_prompt, x_sample, c_prompt, c_sample, cache_mla, cache_diff, cache_nsa, state_nsa_win, page_table,
           ada_w, ada_b, pre_g, post_g, ab_w_in, mla_q_g, mla_w_uq, mla_kv_g, mla_w_uk, mla_w_uv,
           diff_lam, diff_sub_g, ab_w_out, c_w_in, nsa_pe, nsa_w1, nsa_b1, nsa_w2, c_w_out):
    p = dict(ada_w=ada_w, ada_b=ada_b, pre_g=pre_g, post_g=post_g, ab_w_in=ab_w_in,
             mla_q_g=mla_q_g, mla_w_uq=mla_w_uq, mla_kv_g=mla_kv_g, mla_w_uk=mla_w_uk, mla_w_uv=mla_w_uv,
             diff_lam=diff_lam, diff_sub_g=diff_sub_g, ab_w_out=ab_w_out, c_w_in=c_w_in,
             nsa_pe=nsa_pe, nsa_w1=nsa_w1, nsa_b1=nsa_b1, nsa_w2=nsa_w2, c_w_out=c_w_out)
    y_prompt, mla_p, diff_p, nsa_p, win_p = _run(x_prompt, c_prompt, p)
    y_sample, mla_s, diff_s, nsa_s, win_s = _run(x_sample, c_sample, p, cache_mla, cache_diff, cache_nsa,
                                                 state_nsa_win, page_table)
    return (y_prompt, y_sample, mla_p, mla_s, diff_p, diff_s, nsa_p, nsa_s, win_p, win_s)
```
